```python
import jax, jax.numpy as jnp
from jax import lax
import numpy as np

D_MODEL = 1024
BATCH = 8
SEQ = 2048
DEPTH = 2
DEC_BATCH = 128
DEC_SEQ = 4
PAST_LEN = 16384
PAGE_SIZE = 128

N_MIXERS = 2
N_CONV_LAYERS = (DEPTH + 1) // 2
N_SSM_LAYERS = DEPTH // 2
CONV_WIDTH = 31
CONV_BUF = CONV_WIDTH - 1
SSM_GROUP = 16
SSM_GROUPS = D_MODEL // SSM_GROUP
SSM_STATE = 64
SCAN_BLOCK = 128
D_FF = 4 * D_MODEL
N_MEM = 256
XATTN_HEADS = 4
XATTN_HEAD_DIM = D_MODEL // XATTN_HEADS
EPS = 1e-6
DT_MIN = 1e-3
DT_MAX = 1e-1

kernel_name = 'hybrid_conformer_conv_s5_memxattn_step'


def _rmsnorm(x, g):
    x32 = x.astype(jnp.float32)
    y = x32 * lax.rsqrt(jnp.mean(x32 * x32, axis=-1, keepdims=True) + EPS)
    return (y * g.astype(jnp.float32)).astype(x.dtype)


def _conformer_conv(u, buf, w_in, b_in, dw, dw_b, ln_g, ln_b, w_out):
    h = jnp.einsum('bld,de->ble', u, w_in) + b_in
    g = h[..., :D_MODEL] * jax.nn.sigmoid(h[..., D_MODEL:])
    padded = jnp.concatenate([buf.astype(g.dtype), g], axis=1)
    c = lax.conv_general_dilated(
        padded, dw[:, None, :].astype(g.dtype), window_strides=(1,), padding='VALID',
        dimension_numbers=('NWC', 'WIO', 'NWC'), feature_group_count=D_MODEL) + dw_b
    c32 = c.astype(jnp.float32)
    mu = jnp.mean(c32, axis=-1, keepdims=True)
    var = jnp.mean(jnp.square(c32 - mu), axis=-1, keepdims=True)
    n = (c32 - mu) * lax.rsqrt(var + EPS) * ln_g.astype(jnp.float32) + ln_b.astype(jnp.float32)
    s = jax.nn.silu(n).astype(u.dtype)
    out = jnp.einsum('bld,de->ble', s, w_out)
    return out, padded[:, -CONV_BUF:]


def _cplx_affine_combine(e1, e2):
    a1r, a1i, b1r, b1i = e1
    a2r, a2i, b2r, b2i = e2
    return (a2r * a1r - a2i * a1i,
            a2r * a1i + a2i * a1r,
            a2r * b1r - a2i * b1i + b2r,
            a2r * b1i + a2i * b1r + b2i)


def _s5(u, h0_re, h0_im, a_re, a_im, log_dt, b_re, b_im, c_re, c_im, d_skip, w_glu):
    f32 = jnp.float32
    bsz, seq_len, _ = u.shape
    blk = SCAN_BLOCK if seq_len % SCAN_BLOCK == 0 else seq_len
    n_blk = seq_len // blk
    a_re = a_re.astype(f32)
    a_im = a_im.astype(f32)
    b_re = b_re.astype(f32)
    b_im = b_im.astype(f32)
    c_re = c_re.astype(f32)
    c_im = c_im.astype(f32)
    dt = jnp.exp(log_dt.astype(f32))[:, None]
    mag = jnp.exp(dt * a_re)
    ab_re = mag * jnp.cos(dt * a_im)
    ab_im = mag * jnp.sin(dt * a_im)
    den = a_re * a_re + a_im * a_im
    num_re = ab_re - 1.0
    coef_re = (num_re * a_re + ab_im * a_im) / den
    coef_im = (ab_im * a_re - num_re * a_im) / den
    bb_re = coef_re[..., None] * b_re - coef_im[..., None] * b_im
    bb_im = coef_re[..., None] * b_im + coef_im[..., None] * b_re
    d = d_skip.astype(f32).reshape(SSM_GROUPS, SSM_GROUP)
    ug = u.astype(f32).reshape(bsz, n_blk, blk, SSM_GROUPS, SSM_GROUP).transpose(1, 0, 2, 3, 4)

    def block_step(carry, u_blk):
        h_re, h_im = carry
        bu_re = jnp.einsum('bcgi,gpi->bcgp', u_blk, bb_re)
        bu_im = jnp.einsum('bcgi,gpi->bcgp', u_blk, bb_im)
        bu_re = bu_re.at[:, 0].add(ab_re * h_re - ab_im * h_im)
        bu_im = bu_im.at[:, 0].add(ab_re * h_im + ab_im * h_re)
        a_r = jnp.broadcast_to(ab_re, bu_re.shape)
        a_i = jnp.broadcast_to(ab_im, bu_im.shape)
        _, _, s_re, s_im = lax.associative_scan(_cplx_affine_combine, (a_r, a_i, bu_re, bu_im), axis=1)
        y = (jnp.einsum('bcgp,gip->bcgi', s_re, c_re)
             - jnp.einsum('bcgp,gip->bcgi', s_im, c_im) + d * u_blk)
        return (s_re[:, -1], s_im[:, -1]), y

    (h_re, h_im), ys = lax.scan(block_step, (h0_re.astype(f32), h0_im.astype(f32)), ug)
    y = ys.transpose(1, 0, 2, 3, 4).reshape(bsz, seq_len, D_MODEL)
    z = jnp.einsum('bld,de->ble', jax.nn.gelu(y).astype(u.dtype), w_glu)
    out = z[..., :D_MODEL] * jax.nn.sigmoid(z[..., D_MODEL:])
    return out, h_re, h_im


def _mem_kv(mem, w_k, w_v):
    bsz = mem.shape[0]
    k = jnp.einsum('bmd,de->bme', mem, w_k).reshape(bsz, N_MEM, XATTN_HEADS, XATTN_HEAD_DIM)
    v = jnp.einsum('bmd,de->bme', mem, w_v).reshape(bsz, N_MEM, XATTN_HEADS, XATTN_HEAD_DIM)
    return k, v


def _cross_attention(xn, mem_k, mem_v, w_q, w_o):
    bsz, seq_len, _ = xn.shape
    q = jnp.einsum('bld,de->ble', xn, w_q).reshape(bsz, seq_len, XATTN_HEADS, XATTN_HEAD_DIM)
    s = jnp.einsum('blhe,bmhe->bhlm', q.astype(jnp.float32), mem_k.astype(jnp.float32)) * (XATTN_HEAD_DIM ** -0.5)
    p = jax.nn.softmax(s, axis=-1)
    o = jnp.einsum('bhlm,bmhe->blhe', p, mem_v.astype(jnp.float32)).reshape(bsz, seq_len, D_MODEL)
    return jnp.einsum('bld,de->ble', o.astype(xn.dtype), w_o)


def _sqrelu_mlp(xn, w_up, w_down):
    h = jax.nn.relu(jnp.einsum('bld,df->blf', xn, w_up))
    return jnp.einsum('blf,fd->bld', h * h, w_down)


def _trunk(x, conv_bufs, s5_re, s5_im, mem_k, mem_v, p):
    new_conv, new_re, new_im = [], [], []
    for i in range(DEPTH):
        j = i // N_MIXERS
        h = _rmsnorm(x, p['norm_mix'][i])
        if i % N_MIXERS == 0:
            out, nb = _conformer_conv(h, conv_bufs[j], p['conv_w_in'][j], p['conv_b_in'][j], p['conv_dw'][j],
                                      p['conv_dw_b'][j], p['conv_ln_g'][j], p['conv_ln_b'][j], p['conv_w_out'][j])
            new_conv.append(nb)
        else:
            out, hr, hi = _s5(h, s5_re[j], s5_im[j], p['s5_a_re'][j], p['s5_a_im'][j], p['s5_log_dt'][j],
                              p['s5_b_re'][j], p['s5_b_im'][j], p['s5_c_re'][j], p['s5_c_im'][j],
                              p['s5_d'][j], p['s5_w_glu'][j])
            new_re.append(hr)
            new_im.append(hi)
        x = x + out
        x = x + _cross_attention(_rmsnorm(x, p['norm_xattn'][i]), mem_k[i], mem_v[i],
                                 p['xattn_w_q'][i], p['xattn_w_o'][i])
        x = x + _sqrelu_mlp(_rmsnorm(x, p['norm_ffn'][i]), p['mlp_w_up'][i], p['mlp_w_down'][i])
    y = _rmsnorm(x, p['norm_final'])
    return y, jnp.stack(new_conv), jnp.stack(new_re), jnp.stack(new_im)


def setup_inputs(seed: int = 0) -> dict:
    key = jax.random.key(seed)
    ks = list(jax.random.split(key, 40))
    f32 = jnp.float32

    def nrm(shape, scale):
        return jax.random.normal(ks.pop(), shape, f32) * scale

    inp = {}
    inp['x_prompt'] = nrm((BATCH, SEQ, D_MODEL), 1.0)
    inp['x_sample'] = nrm((DEC_BATCH, DEC_SEQ, D_MODEL), 1.0)
    inp['mem_prompt'] = nrm((BATCH, N_MEM, D_MODEL), 1.0)
    inp['cache_conv'] = nrm((N_CONV_LAYERS, DEC_BATCH, CONV_BUF, D_MODEL), 0.5)
    inp['state_s5_re'] = nrm((N_SSM_LAYERS, DEC_BATCH, SSM_GROUPS, SSM_STATE), 0.1)
    inp['state_s5_im'] = nrm((N_SSM_LAYERS, DEC_BATCH, SSM_GROUPS, SSM_STATE), 0.1)
    inp['cache_mem_k'] = nrm((DEPTH, DEC_BATCH, N_MEM, XATTN_HEADS, XATTN_HEAD_DIM), 1.0)
    inp['cache_mem_v'] = nrm((DEPTH, DEC_BATCH, N_MEM, XATTN_HEADS, XATTN_HEAD_DIM), 1.0)
    inp['norm_mix'] = 1.0 + nrm((DEPTH, D_MODEL), 0.01)
    inp['norm_xattn'] = 1.0 + nrm((DEPTH, D_MODEL), 0.01)
    inp['norm_ffn'] = 1.0 + nrm((DEPTH, D_MODEL), 0.01)
    inp['norm_final'] = 1.0 + nrm((D_MODEL,), 0.01)
    inp['conv_w_in'] = nrm((N_CONV_LAYERS, D_MODEL, 2 * D_MODEL), D_MODEL ** -0.5)
    inp['conv_b_in'] = nrm((N_CONV_LAYERS, 2 * D_MODEL), 0.01)
    inp['conv_dw'] = nrm((N_CONV_LAYERS, CONV_WIDTH, D_MODEL), CONV_WIDTH ** -0.5)
    inp['conv_dw_b'] = nrm((N_CONV_LAYERS, D_MODEL), 0.01)
    inp['conv_ln_g'] = 1.0 + nrm((N_CONV_LAYERS, D_MODEL), 0.01)
    inp['conv_ln_b'] = nrm((N_CONV_LAYERS, D_MODEL), 0.01)
    inp['conv_w_out'] = nrm((N_CONV_LAYERS, D_MODEL, D_MODEL), D_MODEL ** -0.5)
    inp['s5_a_re'] = -0.5 + nrm((N_SSM_LAYERS, SSM_GROUPS, SSM_STATE), 0.01)
    inp['s5_a_im'] = jnp.pi * jnp.arange(SSM_STATE, dtype=f32) + nrm((N_SSM_LAYERS, SSM_GROUPS, SSM_STATE), 0.01)
    inp['s5_log_dt'] = jax.random.uniform(ks.pop(), (N_SSM_LAYERS, SSM_GROUPS), f32,
                                          float(np.log(DT_MIN)), float(np.log(DT_MAX)))
    inp['s5_b_re'] = nrm((N_SSM_LAYERS, SSM_GROUPS, SSM_STATE, SSM_GROUP), (2 * SSM_GROUP) ** -0.5)
    inp['s5_b_im'] = nrm((N_SSM_LAYERS, SSM_GROUPS, SSM_STATE, SSM_GROUP), (2 * SSM_GROUP) ** -0.5)
    inp['s5_c_re'] = nrm((N_SSM_LAYERS, SSM_GROUPS, SSM_GROUP, SSM_STATE), SSM_STATE ** -0.5)
    inp['s5_c_im'] = nrm((N_SSM_LAYERS, SSM_GROUPS, SSM_GROUP, SSM_STATE), SSM_STATE ** -0.5)
    inp['s5_d'] = nrm((N_SSM_LAYERS, D_MODEL), 1.0)
    inp['s5_w_glu'] = nrm((N_SSM_LAYERS, D_MODEL, 2 * D_MODEL), D_MODEL ** -0.5)
    inp['xattn_w_q'] = nrm((DEPTH, D_MODEL, D_MODEL), D_MODEL ** -0.5)
    inp['xattn_w_k'] = nrm((DEPTH, D_MODEL, D_MODEL), D_MODEL ** -0.5)
    inp['xattn_w_v'] = nrm((DEPTH, D_MODEL, D_MODEL), D_MODEL ** -0.5)
    inp['xattn_w_o'] = nrm((DEPTH, D_MODEL, D_MODEL), D_MODEL ** -0.5)
    inp['mlp_w_up'] = nrm((DEPTH, D_MODEL, D_FF), D_MODEL ** -0.5)
    inp['mlp_w_down'] = nrm((DEPTH, D_FF, D_MODEL), D_FF ** -0.5)
    return inp


def reference(x_prompt, x_sample, mem_prompt, cache_conv, state_s5_re, state_s5_im, cache_mem_k, cache_mem_v,
              norm_mix, norm_xattn, norm_ffn, norm_final,
              conv_w_in, conv_b_in, conv_dw, conv_dw_b, conv_ln_g, conv_ln_b, conv_w_out,
              s5_a_re, s5_a_im, s5_log_dt, s5_b_re, s5_b_im, s5_c_re, s5_c_im, s5_d, s5_w_glu,
              xattn_w_q, xattn_w_k, xattn_w_v, xattn_w_o, mlp_w_up, mlp_w_down):
    p = {'norm_mix': norm_mix, 'norm_xattn': norm_xattn, 'norm_ffn': norm_ffn, 'norm_final': norm_final,
         'conv_w_in': conv_w_in, 'conv_b_in': conv_b_in, 'conv_dw': conv_dw, 'conv_dw_b': conv_dw_b,
         'conv_ln_g': conv_ln_g, 'conv_ln_b': conv_ln_b, 'conv_w_out': conv_w_out,
         's5_a_re': s5_a_re, 's5_a_im': s5_a_im, 's5_log_dt': s5_log_dt, 's5_b_re': s5_b_re,
         's5_b_im': s5_b_im, 's5_c_re': s5_c_re, 's5_c_im': s5_c_im, 's5_d': s5_d, 's5_w_glu': s5_w_glu,
         'xattn_w_q': xattn_w_q, 'xattn_w_o': xattn_w_o, 'mlp_w_up': mlp_w_up, 'mlp_w_down': mlp_w_down}
    bp = x_prompt.shape[0]
    kv = [_mem_kv(mem_prompt, xattn_w_k[i], xattn_w_v[i]) for i in range(DEPTH)]
    mem_k_prompt = jnp.stack([kv[i][0] for i in range(DEPTH)])
    mem_v_prompt = jnp.stack([kv[i][1] for i in range(DEPTH)])
    conv0 = jnp.zeros((N_CONV_LAYERS, bp, CONV_BUF, D_MODEL), x_prompt.dtype)
    s50 = jnp.zeros((N_SSM_LAYERS, bp, SSM_GROUPS, SSM_STATE), jnp.float32)
    y_prompt, conv_prompt, s5_re_prompt, s5_im_prompt = _trunk(
        x_prompt, conv0, s50, s50, mem_k_prompt, mem_v_prompt, p)
    y_sample, conv_sample, s5_re_sample, s5_im_sample = _trunk(
        x_sample, cache_conv, state_s5_re, state_s5_im, cache_mem_k, cache_mem_v, p)
    return (y_prompt, y_sample, conv_prompt, conv_sample, s5_re_prompt, s5_im_prompt,
            s5_re_sample, s5_im_sample, mem_k_prompt, mem_v_prompt)
```

```python
import functools
import math

import jax
import jax.numpy as jnp
from jax import lax
from jax.experimental import pallas as pl
from jax.experimental.pallas import tpu as pltpu

F32 = jnp.float32
BF16 = jnp.bfloat16

D = 1024
DEPTH = 2
D_FF = 4 * D
N_MEM = 256
HEADS = 4
HD = D // HEADS
CONV_W = 31
CONV_BUF = CONV_W - 1
HIST_PAD = 32
SSM_GROUP = 16
SSM_GROUPS = D // SSM_GROUP
SSM_STATE = 64
SSM_LANES = SSM_GROUPS * SSM_STATE
S5_CHUNKS = 4
S5_CW = SSM_LANES // S5_CHUNKS
EPS = 1e-6
ATTN_SCALE = HD ** -0.5
NT_DIMS = (((1,), (1,)), ((), ()))
MIB = 2 ** 20


def _dot(a, b):
    return jnp.dot(a, b, preferred_element_type=F32)


def _rms(x, g):
    ms = jnp.mean(x * x, axis=-1, keepdims=True)
    return x * lax.rsqrt(ms + EPS) * g


def _softmax(s):
    m = jnp.max(s, axis=-1, keepdims=True)
    e = jnp.exp(s - m)
    return e / jnp.sum(e, axis=-1, keepdims=True)


def _const_spec(shape):
    nd = len(shape)
    return pl.BlockSpec(shape, lambda *_: (0,) * nd, pipeline_mode=pl.Buffered(1))


def _params(sem, vmem_mib):
    return pltpu.CompilerParams(dimension_semantics=sem, vmem_limit_bytes=vmem_mib * MIB)


def _kv_kernel(mem_ref, wk_ref, wv_ref, k_ref, v_ref):
    m = mem_ref[...].astype(BF16)
    k_ref[0] = _dot(m, wk_ref[0])
    v_ref[0] = _dot(m, wv_ref[0])


def _mem_kv(mem2d, wk, wv):
    rows = mem2d.shape[0]
    tm = 512
    out = jax.ShapeDtypeStruct((DEPTH, rows, D), F32)
    return pl.pallas_call(
        _kv_kernel,
        grid=(DEPTH, rows // tm),
        in_specs=[pl.BlockSpec((tm, D), lambda l, r: (r, 0)),
                  pl.BlockSpec((1, D, D), lambda l, r: (l, 0, 0)),
                  pl.BlockSpec((1, D, D), lambda l, r: (l, 0, 0))],
        out_specs=[pl.BlockSpec((1, tm, D), lambda l, r: (l, r, 0)),
                   pl.BlockSpec((1, tm, D), lambda l, r: (l, r, 0))],
        out_shape=[out, out],
        compiler_params=_params(("arbitrary", "arbitrary"), 32),
        name="mem_kv",
    )(mem2d, wk, wv)


MLP_CH = 1024


def _mlp_kernel(x_ref, g_ref, wup_ref, wdn_ref, gfin_ref, o_ref, xn_sc, acc_sc, *, n_inner, final_norm):
    c = pl.program_id(1)

    @pl.when(c == 0)
    def _():
        x = x_ref[...]
        xn_sc[...] = _rms(x, g_ref[...]).astype(BF16)
        acc_sc[...] = x

    xn = xn_sc[...]
    acc = acc_sc[...]
    for j in range(n_inner):
        h = _dot(xn, wup_ref[:, j * MLP_CH:(j + 1) * MLP_CH])
        h = jnp.maximum(h, 0.0)
        acc = acc + _dot((h * h).astype(BF16), wdn_ref[j * MLP_CH:(j + 1) * MLP_CH, :])
    acc_sc[...] = acc

    @pl.when(c == pl.num_programs(1) - 1)
    def _():
        if final_norm:
            o_ref[...] = _rms(acc, gfin_ref[...])
        else:
            o_ref[...] = acc


def _mlp(x2d, g, wup, wdn, gfin, *, tm, hid_steps, final_norm):
    rows = x2d.shape[0]
    hid_blk = D_FF // hid_steps
    wmode = dict(pipeline_mode=pl.Buffered(1)) if hid_steps == 1 else {}
    kern = functools.partial(_mlp_kernel, n_inner=hid_blk // MLP_CH, final_norm=final_norm)
    return pl.pallas_call(
        kern,
        grid=(rows // tm, hid_steps),
        in_specs=[pl.BlockSpec((tm, D), lambda r, c: (r, 0)),
                  _const_spec((1, D)),
                  pl.BlockSpec((D, hid_blk), lambda r, c: (0, c), **wmode),
                  pl.BlockSpec((hid_blk, D), lambda r, c: (c, 0), **wmode),
                  _const_spec((1, D))],
        out_specs=pl.BlockSpec((tm, D), lambda r, c: (r, 0)),
        out_shape=jax.ShapeDtypeStruct((rows, D), F32),
        scratch_shapes=[pltpu.VMEM((tm, D), BF16), pltpu.VMEM((tm, D), F32)],
        compiler_params=_params(("arbitrary", "arbitrary"), 48),
        name="mlp",
    )(x2d, g, wup, wdn, gfin)


def _attn_prompt_kernel(x_ref, g_ref, wq_ref, k_ref, v_ref, wo_ref, o_ref):
    x = x_ref[0]
    xn = _rms(x, g_ref[...]).astype(BF16)
    q = (_dot(xn, wq_ref[...]) * ATTN_SCALE).astype(BF16)
    kb = k_ref[0, 0].astype(BF16)
    vb = v_ref[0, 0].astype(BF16)
    outs = []
    for h in range(HEADS):
        sl = slice(h * HD, (h + 1) * HD)
        s = lax.dot_general(q[:, sl], kb[:, sl], NT_DIMS, preferred_element_type=F32)
        p = _softmax(s)
        outs.append(_dot(p.astype(BF16), vb[:, sl]).astype(BF16))
    o = jnp.concatenate(outs, axis=-1)
    o_ref[0] = x + _dot(o, wo_ref[...])


def _attn_prompt(x3d, g, wq, k4d, v4d, wo, layer, *, tq):
    b, l, _ = x3d.shape
    return pl.pallas_call(
        _attn_prompt_kernel,
        grid=(b, l // tq),
        in_specs=[pl.BlockSpec((1, tq, D), lambda i, t: (i, t, 0)),
                  _const_spec((1, D)),
                  _const_spec((D, D)),
                  pl.BlockSpec((1, 1, N_MEM, D), lambda i, t: (layer, i, 0, 0)),
                  pl.BlockSpec((1, 1, N_MEM, D), lambda i, t: (layer, i, 0, 0)),
                  _const_spec((D, D))],
        out_specs=pl.BlockSpec((1, tq, D), lambda i, t: (i, t, 0)),
        out_shape=jax.ShapeDtypeStruct(x3d.shape, F32),
        compiler_params=_params(("arbitrary", "arbitrary"), 40),
        name="attn_prompt",
    )(x3d, g, wq, k4d, v4d, wo)


def _attn_sample_kernel(x_ref, g_ref, wq_ref, k_ref, v_ref, wo_ref, o_ref, q_sc, o_sc, *, sb, seq_len):
    i = pl.program_id(0)

    @pl.when(i == 0)
    def _():
        xn = _rms(x_ref[...], g_ref[...]).astype(BF16)
        q_sc[...] = _dot(xn, wq_ref[...]) * ATTN_SCALE

    pair_rows = 2 * seq_len
    first_seq_row = lax.broadcasted_iota(jnp.int32, (pair_rows, HD), 0) < seq_len

    def pair(j, carry):
        r0 = pl.multiple_of((i * sb + 2 * j) * seq_len, pair_rows)
        q8 = q_sc[pl.ds(r0, pair_rows), :].astype(BF16)
        res = []
        for h in range(HEADS):
            sl = slice(h * HD, (h + 1) * HD)
            oh = []
            for s in range(2):
                kb = k_ref[0, 2 * j + s, :, sl].astype(BF16)
                vb = v_ref[0, 2 * j + s, :, sl].astype(BF16)
                sc = lax.dot_general(q8[:, sl], kb, NT_DIMS, preferred_element_type=F32)
                oh.append(_dot(_softmax(sc).astype(BF16), vb))
            res.append(jnp.where(first_seq_row, oh[0], oh[1]))
        o_sc[pl.ds(r0, pair_rows), :] = jnp.concatenate(res, axis=-1)
        return carry

    lax.fori_loop(0, sb // 2, pair, 0)

    @pl.when(i == pl.num_programs(0) - 1)
    def _():
        o_ref[...] = x_ref[...] + _dot(o_sc[...].astype(BF16), wo_ref[...])


def _attn_sample(x2d, g, wq, k4d, v4d, wo, layer, *, sb, seq_len):
    rows = x2d.shape[0]
    nseq = rows // seq_len
    assert 2 * seq_len == 8 and nseq % sb == 0 and sb % 2 == 0
    kern = functools.partial(_attn_sample_kernel, sb=sb, seq_len=seq_len)
    return pl.pallas_call(
        kern,
        grid=(nseq // sb,),
        in_specs=[_const_spec((rows, D)),
                  _const_spec((1, D)),
                  _const_spec((D, D)),
                  pl.BlockSpec((1, sb, N_MEM, D), lambda i: (layer, i, 0, 0)),
                  pl.BlockSpec((1, sb, N_MEM, D), lambda i: (layer, i, 0, 0)),
                  _const_spec((D, D))],
        out_specs=pl.BlockSpec((rows, D), lambda i: (0, 0)),
        out_shape=jax.ShapeDtypeStruct((rows, D), F32),
        scratch_shapes=[pltpu.VMEM((rows, D), F32), pltpu.VMEM((rows, D), F32)],
        compiler_params=_params(("arbitrary",), 48),
        name="attn_sample",
    )(x2d, g, wq, k4d, v4d, wo)


def _glu_in(x, g_ref, win_ref, bin_ref):
    xn = _rms(x, g_ref[...]).astype(BF16)
    h = _dot(xn, win_ref[...]) + bin_ref[...]
    return h[:, :D] * jax.nn.sigmoid(h[:, D:])


def _ln_silu_out(c, lng_ref, lnb_ref, wout_ref):
    mu = jnp.mean(c, axis=-1, keepdims=True)
    cc = c - mu
    var = jnp.mean(cc * cc, axis=-1, keepdims=True)
    n = cc * lax.rsqrt(var + EPS) * lng_ref[...] + lnb_ref[...]
    s = (n * jax.nn.sigmoid(n)).astype(BF16)
    return _dot(s, wout_ref[...])


def _conv_prompt_kernel(x_ref, g_ref, win_ref, bin_ref, dw_ref, dwb_ref, lng_ref, lnb_ref, wout_ref,
                        o_ref, hist_ref, gbuf, *, tl):
    @pl.when(pl.program_id(1) == 0)
    def _():
        gbuf[0:HIST_PAD, :] = jnp.zeros((HIST_PAD, D), F32)

    x = x_ref[0]
    gbuf[HIST_PAD:HIST_PAD + tl, :] = _glu_in(x, g_ref, win_ref, bin_ref)

    base = HIST_PAD - CONV_BUF
    acc = None
    for r in range(8):
        part = None
        for k in range(r, CONV_W, 8):
            term = dw_ref[k:k + 1, :] * gbuf[base + k:base + k + tl, :]
            part = term if part is None else part + term
        acc = part if acc is None else acc + part
    c = acc + dwb_ref[...]
    o_ref[0] = x + _ln_silu_out(c, lng_ref, lnb_ref, wout_ref)

    tail = gbuf[tl:tl + HIST_PAD, :]
    hist_ref[0] = tail
    gbuf[0:HIST_PAD, :] = tail


def _conv_prompt(x3d, g, win, bin_, dw, dwb, lng, lnb, wout, *, tl):
    b, l, _ = x3d.shape
    kern = functools.partial(_conv_prompt_kernel, tl=tl)
    return pl.pallas_call(
        kern,
        grid=(b, l // tl),
        in_specs=[pl.BlockSpec((1, tl, D), lambda i, t: (i, t, 0)),
                  _const_spec((1, D)),
                  _const_spec((D, 2 * D)),
                  _const_spec((1, 2 * D)),
                  _const_spec((HIST_PAD, D)),
                  _const_spec((1, D)),
                  _const_spec((1, D)),
                  _const_spec((1, D)),
                  _const_spec((D, D))],
        out_specs=[pl.BlockSpec((1, tl, D), lambda i, t: (i, t, 0)),
                   pl.BlockSpec((1, HIST_PAD, D), lambda i, t: (i, 0, 0))],
        out_shape=[jax.ShapeDtypeStruct(x3d.shape, F32),
                   jax.ShapeDtypeStruct((b, HIST_PAD, D), F32)],
        scratch_shapes=[pltpu.VMEM((HIST_PAD + tl, D), F32)],
        compiler_params=_params(("arbitrary", "arbitrary"), 40),
        name="conv_prompt",
    )(x3d, g, win, bin_, dw, dwb, lng, lnb, wout)


def _conv_sample_kernel(x_ref, hist_ref, g_ref, win_ref, bin_ref, dw_ref, dwb_ref, lng_ref, lnb_ref, wout_ref,
                        o_ref, nhist_ref, *, nb, seq_len):
    xt = jnp.concatenate([x_ref[:, t * D:(t + 1) * D] for t in range(seq_len)], axis=0)
    gl = _glu_in(xt, g_ref, win_ref, bin_ref)
    g_t = [gl[t * nb:(t + 1) * nb, :] for t in range(seq_len)]

    def padded(r, lanes):
        if r < CONV_BUF:
            return hist_ref[:, r * D + lanes.start:r * D + lanes.stop]
        return g_t[r - CONV_BUF][:, lanes]

    cs = []
    for t in range(seq_len):
        tiles = []
        for j in range(D // 128):
            lanes = slice(j * 128, (j + 1) * 128)
            acc = None
            for k in range(CONV_W):
                term = dw_ref[k:k + 1, lanes] * padded(t + k, lanes)
                acc = term if acc is None else acc + term
            tiles.append(acc)
        cs.append(jnp.concatenate(tiles, axis=-1))
    c = jnp.concatenate(cs, axis=0) + dwb_ref[...]
    res = _ln_silu_out(c, lng_ref, lnb_ref, wout_ref)
    for t in range(seq_len):
        o_ref[:, t * D:(t + 1) * D] = x_ref[:, t * D:(t + 1) * D] + res[t * nb:(t + 1) * nb, :]
    keep = CONV_BUF - seq_len
    nhist_ref[:, 0:keep * D] = hist_ref[:, seq_len * D:CONV_BUF * D]
    for t in range(seq_len):
        nhist_ref[:, (keep + t) * D:(keep + t + 1) * D] = g_t[t]


def _conv_sample(xv, histv, g, win, bin_, dw, dwb, lng, lnb, wout, *, nb, seq_len):
    nseq = xv.shape[0]
    assert seq_len <= CONV_BUF and nseq % nb == 0 and nb % 8 == 0
    kern = functools.partial(_conv_sample_kernel, nb=nb, seq_len=seq_len)
    return pl.pallas_call(
        kern,
        grid=(nseq // nb,),
        in_specs=[pl.BlockSpec((nb, seq_len * D), lambda i: (i, 0)),
                  pl.BlockSpec((nb, CONV_BUF * D), lambda i: (i, 0)),
                  _const_spec((1, D)),
                  _const_spec((D, 2 * D)),
                  _const_spec((1, 2 * D)),
                  _const_spec((HIST_PAD, D)),
                  _const_spec((1, D)),
                  _const_spec((1, D)),
                  _const_spec((1, D)),
                  _const_spec((D, D))],
        out_specs=[pl.BlockSpec((nb, seq_len * D), lambda i: (i, 0)),
                   pl.BlockSpec((nb, CONV_BUF * D), lambda i: (i, 0))],
        out_shape=[jax.ShapeDtypeStruct(xv.shape, F32),
                   jax.ShapeDtypeStruct(histv.shape, F32)],
        compiler_params=_params(("arbitrary",), 40),
        name="conv_sample",
    )(xv, histv, g, win, bin_, dw, dwb, lng, lnb, wout)


def _s5_scan(hs, st, are_ref, aim_ref, nb, tl):
    for k in range(S5_CHUNKS):
        ar = are_ref[:, k * S5_CW:(k + 1) * S5_CW]
        ai = aim_ref[:, k * S5_CW:(k + 1) * S5_CW]
        cre = slice(2 * k * S5_CW, (2 * k + 1) * S5_CW)
        cim = slice((2 * k + 1) * S5_CW, (2 * k + 2) * S5_CW)

        def seq_group(bg, carry, ar=ar, ai=ai, cre=cre, cim=cim):
            b0 = pl.multiple_of(bg * 8, 8)

            def step(t, h):
                hr, hi = h
                row = pl.multiple_of(t * nb + b0, 8)
                nr = ar * hr - ai * hi + hs[pl.ds(row, 8), cre]
                ni = ar * hi + ai * hr + hs[pl.ds(row, 8), cim]
                hs[pl.ds(row, 8), cre] = nr
                hs[pl.ds(row, 8), cim] = ni
                return nr, ni

            hr, hi = lax.fori_loop(0, tl, step, (st[pl.ds(b0, 8), cre], st[pl.ds(b0, 8), cim]),
                                   unroll=min(tl, 4))
            st[pl.ds(b0, 8), cre] = hr
            st[pl.ds(b0, 8), cim] = hi
            return carry

        lax.fori_loop(0, nb // 8, seq_group, 0)


def _s5_core(ut, ut_b, bblk_ref, are_ref, aim_ref, cblk_ref, d_ref, hs, st, nb, tl):
    for k in range(S5_CHUNKS):
        hs[:, 2 * k * S5_CW:(2 * k + 2) * S5_CW] = _dot(ut_b[:, k * 256:(k + 1) * 256], bblk_ref[k])
    _s5_scan(hs, st, are_ref, aim_ref, nb, tl)
    ys = [_dot(hs[:, 2 * k * S5_CW:(2 * k + 2) * S5_CW].astype(BF16), cblk_ref[k]) for k in range(S5_CHUNKS)]
    y = jnp.concatenate(ys, axis=-1) + d_ref[...] * ut
    return jax.nn.gelu(y, approximate=True).astype(BF16)


def _s5_state_in(st, h0re_ref, h0im_ref):
    for k in range(S5_CHUNKS):
        st[:, 2 * k * S5_CW:(2 * k + 1) * S5_CW] = h0re_ref[:, k * S5_CW:(k + 1) * S5_CW]
        st[:, (2 * k + 1) * S5_CW:(2 * k + 2) * S5_CW] = h0im_ref[:, k * S5_CW:(k + 1) * S5_CW]


def _s5_state_out(st, hre_ref, him_ref):
    for k in range(S5_CHUNKS):
        hre_ref[:, k * S5_CW:(k + 1) * S5_CW] = st[:, 2 * k * S5_CW:(2 * k + 1) * S5_CW]
        him_ref[:, k * S5_CW:(k + 1) * S5_CW] = st[:, (2 * k + 1) * S5_CW:(2 * k + 2) * S5_CW]


def _glu_out(z):
    return z[:, :D] * jax.nn.sigmoid(z[:, D:])


def _s5_prompt_kernel(x_ref, g_ref, perm_ref, permt_ref, bblk_ref, are_ref, aim_ref, cblk_ref, d_ref, wglu_ref,
                      h0re_ref, h0im_ref, o_ref, hre_ref, him_ref, hs, st, *, nb, tl):
    t = pl.program_id(0)

    @pl.when(t == 0)
    def _():
        _s5_state_in(st, h0re_ref, h0im_ref)

    x = x_ref[...].reshape(nb * tl, D)
    u = _rms(x, g_ref[...])
    u_hi = u.astype(BF16)
    u_lo = (u - u_hi.astype(F32)).astype(BF16)
    ut_hi = _dot(perm_ref[...], u_hi)
    ut = ut_hi + _dot(perm_ref[...], u_lo)
    ge = _s5_core(ut, ut_hi.astype(BF16), bblk_ref, are_ref, aim_ref, cblk_ref, d_ref, hs, st, nb, tl)
    ge = _dot(permt_ref[...], ge).astype(BF16)
    out = _glu_out(_dot(ge, wglu_ref[...]))
    o_ref[...] = (x + out).reshape(nb, tl, D)

    @pl.when(t == pl.num_programs(0) - 1)
    def _():
        _s5_state_out(st, hre_ref, him_ref)


def _s5_prompt(x3d, g, perm, permt, bblk, are, aim, cblk, d, wglu, h0re, h0im, *, tl):
    nb, l, _ = x3d.shape
    assert nb % 8 == 0
    rows = nb * tl
    kern = functools.partial(_s5_prompt_kernel, nb=nb, tl=tl)
    st_shape = jax.ShapeDtypeStruct((nb, SSM_LANES), F32)
    return pl.pallas_call(
        kern,
        grid=(l // tl,),
        in_specs=[pl.BlockSpec((nb, tl, D), lambda t: (0, t, 0)),
                  _const_spec((1, D)),
                  _const_spec((rows, rows)),
                  _const_spec((rows, rows)),
                  _const_spec((S5_CHUNKS, 256, 2 * S5_CW)),
                  _const_spec((8, SSM_LANES)),
                  _const_spec((8, SSM_LANES)),
                  _const_spec((S5_CHUNKS, 2 * S5_CW, 256)),
                  _const_spec((1, D)),
                  _const_spec((D, 2 * D)),
                  _const_spec((nb, SSM_LANES)),
                  _const_spec((nb, SSM_LANES))],
        out_specs=[pl.BlockSpec((nb, tl, D), lambda t: (0, t, 0)),
                   pl.BlockSpec((nb, SSM_LANES), lambda t: (0, 0)),
                   pl.BlockSpec((nb, SSM_LANES), lambda t: (0, 0))],
        out_shape=[jax.ShapeDtypeStruct(x3d.shape, F32), st_shape, st_shape],
        scratch_shapes=[pltpu.VMEM((rows, 2 * SSM_LANES), F32), pltpu.VMEM((nb, 2 * SSM_LANES), F32)],
        compiler_params=_params(("arbitrary",), 48),
        name="s5_prompt",
    )(x3d, g, perm, permt, bblk, are, aim, cblk, d, wglu, h0re, h0im)


def _s5_sample_kernel(x_ref, g_ref, bblk_ref, are_ref, aim_ref, cblk_ref, d_ref, wglu_ref,
                      h0re_ref, h0im_ref, o_ref, hre_ref, him_ref, hs, st, *, nb, seq_len):
    _s5_state_in(st, h0re_ref, h0im_ref)
    xt = jnp.concatenate([x_ref[:, t * D:(t + 1) * D] for t in range(seq_len)], axis=0)
    ut = _rms(xt, g_ref[...])
    ge = _s5_core(ut, ut.astype(BF16), bblk_ref, are_ref, aim_ref, cblk_ref, d_ref, hs, st, nb, seq_len)
    out = _glu_out(_dot(ge, wglu_ref[...]))
    for t in range(seq_len):
        o_ref[:, t * D:(t + 1) * D] = x_ref[:, t * D:(t + 1) * D] + out[t * nb:(t + 1) * nb, :]
    _s5_state_out(st, hre_ref, him_ref)


def _s5_sample(xv, g, bblk, are, aim, cblk, d, wglu, h0re, h0im, *, nb, seq_len):
    nseq = xv.shape[0]
    assert nseq % nb == 0 and nb % 8 == 0
    rows = nb * seq_len
    kern = functools.partial(_s5_sample_kernel, nb=nb, seq_len=seq_len)
    st_shape = jax.ShapeDtypeStruct((nseq, SSM_LANES), F32)
    st_spec = pl.BlockSpec((nb, SSM_LANES), lambda i: (i, 0))
    return pl.pallas_call(
        kern,
        grid=(nseq // nb,),
        in_specs=[pl.BlockSpec((nb, seq_len * D), lambda i: (i, 0)),
                  _const_spec((1, D)),
                  _const_spec((S5_CHUNKS, 256, 2 * S5_CW)),
                  _const_spec((8, SSM_LANES)),
                  _const_spec((8, SSM_LANES)),
                  _const_spec((S5_CHUNKS, 2 * S5_CW, 256)),
                  _const_spec((1, D)),
                  _const_spec((D, 2 * D)),
                  st_spec, st_spec],
        out_specs=[pl.BlockSpec((nb, seq_len * D), lambda i: (i, 0)), st_spec, st_spec],
        out_shape=[jax.ShapeDtypeStruct(xv.shape, F32), st_shape, st_shape],
        scratch_shapes=[pltpu.VMEM((rows, 2 * SSM_LANES), F32), pltpu.VMEM((nb, 2 * SSM_LANES), F32)],
        compiler_params=_params(("arbitrary",), 48),
        name="s5_sample",
    )(xv, g, bblk, are, aim, cblk, d, wglu, h0re, h0im)


def _s5_discretize(a_re, a_im, log_dt, b_re, b_im, c_re, c_im):
    dt = jnp.exp(log_dt)[:, None]
    mag = jnp.exp(dt * a_re)
    ab_re = mag * jnp.cos(dt * a_im)
    ab_im = mag * jnp.sin(dt * a_im)
    den = a_re * a_re + a_im * a_im
    num_re = ab_re - 1.0
    coef_re = (num_re * a_re + ab_im * a_im) / den
    coef_im = (ab_im * a_re - num_re * a_im) / den
    bb_re = coef_re[..., None] * b_re - coef_im[..., None] * b_im
    bb_im = coef_re[..., None] * b_im + coef_im[..., None] * b_re
    gpc = SSM_GROUPS // S5_CHUNKS
    eye = jnp.eye(gpc, dtype=F32)
    bb = jnp.stack([bb_re, bb_im]).reshape(2, S5_CHUNKS, gpc, SSM_STATE, SSM_GROUP)
    bblk = jnp.einsum('rkgpi,gh->kgirhp', bb, eye).reshape(S5_CHUNKS, gpc * SSM_GROUP, 2 * S5_CW)
    cc = jnp.stack([c_re, -c_im]).reshape(2, S5_CHUNKS, gpc, SSM_GROUP, SSM_STATE)
    cblk = jnp.einsum('rkgip,gh->krgphi', cc, eye).reshape(S5_CHUNKS, 2 * S5_CW, gpc * SSM_GROUP)
    are = jnp.broadcast_to(ab_re.reshape(1, SSM_LANES), (8, SSM_LANES))
    aim = jnp.broadcast_to(ab_im.reshape(1, SSM_LANES), (8, SSM_LANES))
    return bblk.astype(BF16), cblk.astype(BF16), are, aim


def _row_permutation(nb, tl):
    rows = nb * tl
    dst = lax.broadcasted_iota(jnp.int32, (rows, rows), 0)
    src = lax.broadcasted_iota(jnp.int32, (rows, rows), 1)
    perm = ((dst // nb == src % tl) & (dst % nb == src // tl)).astype(BF16)
    return perm, perm.T


S5_TL = 32
CONV_TL = 256
ATTN_TQ = 512
MLP_TM = 512


def kernel(x_prompt, x_sample, mem_prompt, cache_conv, state_s5_re, state_s5_im, cache_mem_k, cache_mem_v,
           norm_mix, norm_xattn, norm_ffn, norm_final,
           conv_w_in, conv_b_in, conv_dw, conv_dw_b, conv_ln_g, conv_ln_b, conv_w_out,
           s5_a_re, s5_a_im, s5_log_dt, s5_b_re, s5_b_im, s5_c_re, s5_c_im, s5_d, s5_w_glu,
           xattn_w_q, xattn_w_k, xattn_w_v, xattn_w_o, mlp_w_up, mlp_w_down):
    bp, lp, _ = x_prompt.shape
    bs, ls, _ = x_sample.shape
    row = lambda v: v.reshape(1, -1)

    wq, wk, wv, wo = (w.astype(BF16) for w in (xattn_w_q, xattn_w_k, xattn_w_v, xattn_w_o))
    wup, wdn = mlp_w_up.astype(BF16), mlp_w_down.astype(BF16)
    conv_p = (conv_w_in[0].astype(BF16), row(conv_b_in[0]),
              jnp.pad(conv_dw[0], ((0, HIST_PAD - CONV_W), (0, 0))), row(conv_dw_b[0]),
              row(conv_ln_g[0]), row(conv_ln_b[0]), conv_w_out[0].astype(BF16))
    bblk, cblk, are, aim = _s5_discretize(s5_a_re[0], s5_a_im[0], s5_log_dt[0], s5_b_re[0], s5_b_im[0],
                                          s5_c_re[0], s5_c_im[0])
    s5_p = (bblk, are, aim, cblk, row(s5_d[0]), s5_w_glu[0].astype(BF16))
    gfin = row(norm_final)

    kp, vp = _mem_kv(mem_prompt.reshape(bp * N_MEM, D), wk, wv)
    kp4 = kp.reshape(DEPTH, bp, N_MEM, D)
    vp4 = vp.reshape(DEPTH, bp, N_MEM, D)
    mem_k_prompt = kp.reshape(DEPTH, bp, N_MEM, HEADS, HD)
    mem_v_prompt = vp.reshape(DEPTH, bp, N_MEM, HEADS, HD)

    def mlp(x2d, i, hid_steps, final):
        return _mlp(x2d, row(norm_ffn[i]), wup[i], wdn[i], gfin, tm=MLP_TM, hid_steps=hid_steps, final_norm=final)

    x, hist = _conv_prompt(x_prompt, row(norm_mix[0]), *conv_p, tl=CONV_TL)
    conv_prompt = hist[None, :, HIST_PAD - CONV_BUF:, :]
    x = _attn_prompt(x, row(norm_xattn[0]), wq[0], kp4, vp4, wo[0], 0, tq=ATTN_TQ)
    x = mlp(x.reshape(bp * lp, D), 0, 1, False).reshape(bp, lp, D)
    perm, permt = _row_permutation(bp, S5_TL)
    zeros_state = jnp.zeros((bp, SSM_LANES), F32)
    x, hre, him = _s5_prompt(x, row(norm_mix[1]), perm, permt, *s5_p, zeros_state, zeros_state, tl=S5_TL)
    s5_re_prompt = hre.reshape(1, bp, SSM_GROUPS, SSM_STATE)
    s5_im_prompt = him.reshape(1, bp, SSM_GROUPS, SSM_STATE)
    x = _attn_prompt(x, row(norm_xattn[1]), wq[1], kp4, vp4, wo[1], 1, tq=ATTN_TQ)
    y_prompt = mlp(x.reshape(bp * lp, D), 1, 1, True).reshape(bp, lp, D)

    ck = cache_mem_k.reshape(DEPTH, bs, N_MEM, D)
    cv = cache_mem_v.reshape(DEPTH, bs, N_MEM, D)
    xv, nhist = _conv_sample(x_sample.reshape(bs, ls * D), cache_conv[0].reshape(bs, CONV_BUF * D),
                             row(norm_mix[0]), *conv_p, nb=32, seq_len=ls)
    conv_sample = nhist.reshape(1, bs, CONV_BUF, D)
    x = _attn_sample(xv.reshape(bs * ls, D), row(norm_xattn[0]), wq[0], ck, cv, wo[0], 0, sb=4, seq_len=ls)
    x = mlp(x, 0, 4, False)
    xv, hre, him = _s5_sample(x.reshape(bs, ls * D), row(norm_mix[1]), *s5_p,
                              state_s5_re[0].reshape(bs, SSM_LANES), state_s5_im[0].reshape(bs, SSM_LANES),
                              nb=64, seq_len=ls)
    s5_re_sample = hre.reshape(1, bs, SSM_GROUPS, SSM_STATE)
    s5_im_sample = him.reshape(1, bs, SSM_GROUPS, SSM_STATE)
    x = _attn_sample(xv.reshape(bs * ls, D), row(norm_xattn[1]), wq[1], ck, cv, wo[1], 1, sb=4, seq_len=ls)
    y_sample = mlp(x, 1, 4, True).reshape(bs, ls, D)

    return (y_prompt, y_sample, conv_prompt, conv_sample, s5_re_prompt, s5_im_prompt,
            s5_re_sample, s5_im_sample, mem_k_prompt, mem_v_prompt)
```

```python
import functools

import jax
import jax.numpy as jnp
from jax import lax
from jax.experimental import pallas as pl
from jax.experimental.pallas import tpu as pltpu

F32 = jnp.float32
BF16 = jnp.bfloat16

LANES = 128
D = 1024
DEPTH = 2
D_FF = 4 * D
N_MEM = 256
HEADS = 4
HD = D // HEADS
HD_TILES = HD // LANES
KV_ROWS = N_MEM * HEADS * HD_TILES
CONV_W = 31
CONV_BUF = CONV_W - 1
HIST_PAD = 32
SSM_GROUP = 16
SSM_GROUPS = D // SSM_GROUP
SSM_STATE = 64
SSM_LANES = SSM_GROUPS * SSM_STATE
S5_CHUNKS = 4
S5_CW = SSM_LANES // S5_CHUNKS
EPS = 1e-6
ATTN_SCALE = HD ** -0.5
NT_DIMS = (((1,), (1,)), ((), ()))
MIB = 2 ** 20


def _dot(a, b):
    return jnp.dot(a, b, preferred_element_type=F32)


def _rms(x, g):
    ms = jnp.mean(x * x, axis=-1, keepdims=True)
    return x * lax.rsqrt(ms + EPS) * g


def _softmax(s):
    m = jnp.max(s, axis=-1, keepdims=True)
    e = jnp.exp(s - m)
    return e / jnp.sum(e, axis=-1, keepdims=True)


def _const_spec(shape):
    nd = len(shape)
    return pl.BlockSpec(shape, lambda *_: (0,) * nd, pipeline_mode=pl.Buffered(1))


def _layer_spec(shape, layer):
    nd = len(shape)
    return pl.BlockSpec((None,) + shape, lambda *_: (layer,) + (0,) * nd, pipeline_mode=pl.Buffered(1))


def _params(sem, vmem_mib):
    return pltpu.CompilerParams(dimension_semantics=sem, vmem_limit_bytes=vmem_mib * MIB)


def _kv_rows_view(kv5d):
    lead = kv5d.shape[:-3]
    v = kv5d.reshape(lead + (N_MEM, HEADS, HD_TILES, LANES))
    v = jnp.swapaxes(v, -3, -2)
    return v.reshape(lead + (KV_ROWS, LANES))


def _kv_rows_unview(rows4d):
    lead = rows4d.shape[:-2]
    v = rows4d.reshape(lead + (N_MEM, HD_TILES, HEADS, LANES))
    v = jnp.swapaxes(v, -3, -2)
    return v.reshape(lead + (N_MEM, HEADS, HD))


def _head_rows(half, h, n):
    return pl.ds(half * HEADS + h, n, stride=HEADS * HD_TILES)


def _load_head(ref, lead, h):
    parts = [ref[lead + (_head_rows(e, h, N_MEM), slice(None))] for e in range(HD_TILES)]
    return jnp.concatenate(parts, axis=1).astype(BF16)


def _kv_kernel(mem_ref, wk_ref, wv_ref, k_ref, v_ref, *, tm):
    m = mem_ref[...].astype(BF16)
    for w_ref, o_ref in ((wk_ref, k_ref), (wv_ref, v_ref)):
        y = _dot(m, w_ref[...])
        for h in range(HEADS):
            for e in range(HD_TILES):
                c0 = h * HD + e * LANES
                o_ref[0, _head_rows(e, h, tm), :] = y[:, c0:c0 + LANES]


def _mem_kv(mem2d, wk, wv):
    rows = mem2d.shape[0]
    tm = 512
    rpm = HEADS * HD_TILES
    out = jax.ShapeDtypeStruct((DEPTH, rows * rpm, LANES), F32)
    return pl.pallas_call(
        functools.partial(_kv_kernel, tm=tm),
        grid=(DEPTH, rows // tm),
        in_specs=[pl.BlockSpec((tm, D), lambda l, r: (r, 0)),
                  pl.BlockSpec((None, D, D), lambda l, r: (l, 0, 0)),
                  pl.BlockSpec((None, D, D), lambda l, r: (l, 0, 0))],
        out_specs=[pl.BlockSpec((1, tm * rpm, LANES), lambda l, r: (l, r, 0)),
                   pl.BlockSpec((1, tm * rpm, LANES), lambda l, r: (l, r, 0))],
        out_shape=[out, out],
        compiler_params=_params(("arbitrary", "arbitrary"), 32),
        name="mem_kv",
    )(mem2d, wk, wv)


MLP_CH = 1024


def _mlp_kernel(x_ref, g_ref, wup_ref, wdn_ref, gfin_ref, o_ref, xn_sc, acc_sc, *, layer, n_inner, final_norm):
    c = pl.program_id(1)

    @pl.when(c == 0)
    def _():
        x = x_ref[...]
        xn_sc[...] = _rms(x, g_ref[layer:layer + 1, :]).astype(BF16)
        acc_sc[...] = x

    xn = xn_sc[...]
    acc = acc_sc[...]
    for j in range(n_inner):
        h = _dot(xn, wup_ref[:, j * MLP_CH:(j + 1) * MLP_CH])
        h = jnp.maximum(h, 0.0)
        acc = acc + _dot((h * h).astype(BF16), wdn_ref[j * MLP_CH:(j + 1) * MLP_CH, :])
    acc_sc[...] = acc

    @pl.when(c == pl.num_programs(1) - 1)
    def _():
        if final_norm:
            o_ref[...] = _rms(acc, gfin_ref[...])
        else:
            o_ref[...] = acc


def _mlp(x2d, g, wup, wdn, gfin, layer, *, tm, hid_steps, final_norm):
    rows = x2d.shape[0]
    hid_blk = D_FF // hid_steps
    wmode = dict(pipeline_mode=pl.Buffered(1)) if hid_steps == 1 else {}
    kern = functools.partial(_mlp_kernel, layer=layer, n_inner=hid_blk // MLP_CH, final_norm=final_norm)
    return pl.pallas_call(
        kern,
        grid=(rows // tm, hid_steps),
        in_specs=[pl.BlockSpec((tm, D), lambda r, c: (r, 0)),
                  _const_spec((DEPTH, D)),
                  pl.BlockSpec((None, D, hid_blk), lambda r, c: (layer, 0, c), **wmode),
                  pl.BlockSpec((None, hid_blk, D), lambda r, c: (layer, c, 0), **wmode),
                  _const_spec((1, D))],
        out_specs=pl.BlockSpec((tm, D), lambda r, c: (r, 0)),
        out_shape=jax.ShapeDtypeStruct((rows, D), F32),
        scratch_shapes=[pltpu.VMEM((tm, D), BF16), pltpu.VMEM((tm, D), F32)],
        compiler_params=_params(("arbitrary", "arbitrary"), 48),
        name="mlp",
    )(x2d, g, wup, wdn, gfin)


def _attn_prompt_kernel(x_ref, g_ref, wq_ref, k_ref, v_ref, wo_ref, o_ref, *, layer):
    x = x_ref[0]
    xn = _rms(x, g_ref[layer:layer + 1, :]).astype(BF16)
    q = (_dot(xn, wq_ref[...]) * ATTN_SCALE).astype(BF16)
    outs = []
    for h in range(HEADS):
        kh = _load_head(k_ref, (), h)
        vh = _load_head(v_ref, (), h)
        s = lax.dot_general(q[:, h * HD:(h + 1) * HD], kh, NT_DIMS, preferred_element_type=F32)
        outs.append(_dot(_softmax(s).astype(BF16), vh).astype(BF16))
    o = jnp.concatenate(outs, axis=-1)
    o_ref[0] = x + _dot(o, wo_ref[...])


def _attn_prompt(x3d, g, wq, k4d, v4d, wo, layer, *, tq):
    b, l, _ = x3d.shape
    kv_spec = pl.BlockSpec((None, None, KV_ROWS, LANES), lambda i, t: (layer, i, 0, 0))
    return pl.pallas_call(
        functools.partial(_attn_prompt_kernel, layer=layer),
        grid=(b, l // tq),
        in_specs=[pl.BlockSpec((1, tq, D), lambda i, t: (i, t, 0)),
                  _const_spec((DEPTH, D)),
                  _layer_spec((D, D), layer),
                  kv_spec, kv_spec,
                  _layer_spec((D, D), layer)],
        out_specs=pl.BlockSpec((1, tq, D), lambda i, t: (i, t, 0)),
        out_shape=jax.ShapeDtypeStruct(x3d.shape, F32),
        compiler_params=_params(("arbitrary", "arbitrary"), 40),
        name="attn_prompt",
    )(x3d, g, wq, k4d, v4d, wo)


def _attn_sample_kernel(x_ref, g_ref, wq_ref, k_ref, v_ref, wo_ref, o_ref, q_sc, o_sc, *, layer, sb, seq_len):
    i = pl.program_id(0)

    @pl.when(i == 0)
    def _():
        xn = _rms(x_ref[...], g_ref[layer:layer + 1, :]).astype(BF16)
        q_sc[...] = _dot(xn, wq_ref[...]) * ATTN_SCALE

    pair_rows = 2 * seq_len
    first_seq_row = lax.broadcasted_iota(jnp.int32, (pair_rows, HD), 0) < seq_len

    for j in range(sb // 2):
        r0 = pl.multiple_of((i * sb + 2 * j) * seq_len, pair_rows)
        q8 = q_sc[pl.ds(r0, pair_rows), :].astype(BF16)
        scores = [lax.dot_general(q8[:, h * HD:(h + 1) * HD], _load_head(k_ref, (2 * j + s,), h), NT_DIMS,
                                  preferred_element_type=F32)
                  for s in range(2) for h in range(HEADS)]
        p = _softmax(jnp.concatenate(scores, axis=0))
        res = []
        for h in range(HEADS):
            oh = []
            for s in range(2):
                blk = (s * HEADS + h) * pair_rows
                oh.append(_dot(p[blk:blk + pair_rows, :].astype(BF16), _load_head(v_ref, (2 * j + s,), h)))
            res.append(jnp.where(first_seq_row, oh[0], oh[1]))
        o_sc[pl.ds(r0, pair_rows), :] = jnp.concatenate(res, axis=-1)

    @pl.when(i == pl.num_programs(0) - 1)
    def _():
        o_ref[...] = x_ref[...] + _dot(o_sc[...].astype(BF16), wo_ref[...])


def _attn_sample(x2d, g, wq, k4d, v4d, wo, layer, *, sb, seq_len):
    rows = x2d.shape[0]
    nseq = rows // seq_len
    assert 2 * seq_len == 8 and nseq % sb == 0 and sb % 2 == 0
    kern = functools.partial(_attn_sample_kernel, layer=layer, sb=sb, seq_len=seq_len)
    kv_spec = pl.BlockSpec((None, sb, KV_ROWS, LANES), lambda i: (layer, i, 0, 0))
    return pl.pallas_call(
        kern,
        grid=(nseq // sb,),
        in_specs=[_const_spec((rows, D)),
                  _const_spec((DEPTH, D)),
                  _layer_spec((D, D), layer),
                  kv_spec, kv_spec,
                  _layer_spec((D, D), layer)],
        out_specs=pl.BlockSpec((rows, D), lambda i: (0, 0)),
        out_shape=jax.ShapeDtypeStruct((rows, D), F32),
        scratch_shapes=[pltpu.VMEM((rows, D), F32), pltpu.VMEM((rows, D), F32)],
        compiler_params=_params(("arbitrary",), 48),
        name="attn_sample",
    )(x2d, g, wq, k4d, v4d, wo)


def _glu_in(x, g, win_ref, bin_ref):
    xn = _rms(x, g).astype(BF16)
    h = _dot(xn, win_ref[...]) + bin_ref[...]
    return h[:, :D] * jax.nn.sigmoid(h[:, D:])


def _ln_silu_out(c, lng_ref, lnb_ref, wout_ref):
    mu = jnp.mean(c, axis=-1, keepdims=True)
    cc = c - mu
    var = jnp.mean(cc * cc, axis=-1, keepdims=True)
    n = cc * lax.rsqrt(var + EPS) * lng_ref[...] + lnb_ref[...]
    s = (n * jax.nn.sigmoid(n)).astype(BF16)
    return _dot(s, wout_ref[...])


def _conv_prompt_kernel(x_ref, g_ref, win_ref, bin_ref, dw_ref, dwb_ref, lng_ref, lnb_ref, wout_ref,
                        o_ref, hist_ref, gbuf, *, tl):
    @pl.when(pl.program_id(1) == 0)
    def _():
        gbuf[0:HIST_PAD, :] = jnp.zeros((HIST_PAD, D), F32)

    x = x_ref[0]
    gbuf[HIST_PAD:HIST_PAD + tl, :] = _glu_in(x, g_ref[0:1, :], win_ref, bin_ref)

    base = HIST_PAD - CONV_BUF
    acc = None
    for r in range(8):
        part = None
        for k in range(r, CONV_W, 8):
            term = dw_ref[k:k + 1, :] * gbuf[base + k:base + k + tl, :]
            part = term if part is None else part + term
        acc = part if acc is None else acc + part
    c = acc + dwb_ref[...]
    o_ref[0] = x + _ln_silu_out(c, lng_ref, lnb_ref, wout_ref)

    tail = gbuf[tl:tl + HIST_PAD, :]
    hist_ref[0] = tail
    gbuf[0:HIST_PAD, :] = tail


def _conv_param_specs():
    return [_const_spec((DEPTH, D)),
            _layer_spec((D, 2 * D), 0),
            _const_spec((1, 2 * D)),
            _layer_spec((CONV_W, D), 0),
            _const_spec((1, D)),
            _const_spec((1, D)),
            _const_spec((1, D)),
            _layer_spec((D, D), 0)]


def _conv_prompt(x3d, conv_p, *, tl):
    b, l, _ = x3d.shape
    kern = functools.partial(_conv_prompt_kernel, tl=tl)
    return pl.pallas_call(
        kern,
        grid=(b, l // tl),
        in_specs=[pl.BlockSpec((1, tl, D), lambda i, t: (i, t, 0))] + _conv_param_specs(),
        out_specs=[pl.BlockSpec((1, tl, D), lambda i, t: (i, t, 0)),
                   pl.BlockSpec((1, HIST_PAD, D), lambda i, t: (i, 0, 0))],
        out_shape=[jax.ShapeDtypeStruct(x3d.shape, F32),
                   jax.ShapeDtypeStruct((b, HIST_PAD, D), F32)],
        scratch_shapes=[pltpu.VMEM((HIST_PAD + tl, D), F32)],
        compiler_params=_params(("arbitrary", "arbitrary"), 40),
        name="conv_prompt",
    )(x3d, *conv_p)


def _conv_sample_kernel(x_ref, hist_ref, g_ref, win_ref, bin_ref, dw_ref, dwb_ref, lng_ref, lnb_ref, wout_ref,
                        o_ref, nhist_ref, *, nb, seq_len):
    xt = jnp.concatenate([x_ref[:, t * D:(t + 1) * D] for t in range(seq_len)], axis=0)
    gl = _glu_in(xt, g_ref[0:1, :], win_ref, bin_ref)
    g_t = [gl[t * nb:(t + 1) * nb, :] for t in range(seq_len)]

    def padded(r, lanes):
        if r < CONV_BUF:
            return hist_ref[r, :, lanes]
        return g_t[r - CONV_BUF][:, lanes]

    cs = []
    for t in range(seq_len):
        tiles = []
        for j in range(D // LANES):
            lanes = slice(j * LANES, (j + 1) * LANES)
            acc = None
            for k in range(CONV_W):
                term = dw_ref[k:k + 1, lanes] * padded(t + k, lanes)
                acc = term if acc is None else acc + term
            tiles.append(acc)
        cs.append(jnp.concatenate(tiles, axis=-1))
    c = jnp.concatenate(cs, axis=0) + dwb_ref[...]
    res = _ln_silu_out(c, lng_ref, lnb_ref, wout_ref)
    for t in range(seq_len):
        o_ref[:, t * D:(t + 1) * D] = x_ref[:, t * D:(t + 1) * D] + res[t * nb:(t + 1) * nb, :]
    keep = CONV_BUF - seq_len
    nhist_ref[0:keep] = hist_ref[seq_len:CONV_BUF]
    for t in range(seq_len):
        nhist_ref[keep + t] = g_t[t]


def _conv_sample(xv, hist3, conv_p, *, nb, seq_len):
    nseq = xv.shape[0]
    assert seq_len <= CONV_BUF and nseq % nb == 0 and nb % 8 == 0
    kern = functools.partial(_conv_sample_kernel, nb=nb, seq_len=seq_len)
    hist_spec = pl.BlockSpec((CONV_BUF, nb, D), lambda i: (0, i, 0))
    return pl.pallas_call(
        kern,
        grid=(nseq // nb,),
        in_specs=[pl.BlockSpec((nb, seq_len * D), lambda i: (i, 0)), hist_spec] + _conv_param_specs(),
        out_specs=[pl.BlockSpec((nb, seq_len * D), lambda i: (i, 0)), hist_spec],
        out_shape=[jax.ShapeDtypeStruct(xv.shape, F32),
                   jax.ShapeDtypeStruct(hist3.shape, F32)],
        compiler_params=_params(("arbitrary",), 40),
        name="conv_sample",
    )(xv, hist3, *conv_p)


def _s5_scan(hs, st, are_ref, aim_ref, nb, tl):
    for k in range(S5_CHUNKS):
        ar = are_ref[:, k * S5_CW:(k + 1) * S5_CW]
        ai = aim_ref[:, k * S5_CW:(k + 1) * S5_CW]
        cre = slice(2 * k * S5_CW, (2 * k + 1) * S5_CW)
        cim = slice((2 * k + 1) * S5_CW, (2 * k + 2) * S5_CW)

        def seq_group(bg, carry, ar=ar, ai=ai, cre=cre, cim=cim):
            b0 = pl.multiple_of(bg * 8, 8)

            def step(t, h):
                hr, hi = h
                row = pl.multiple_of(t * nb + b0, 8)
                nr = ar * hr - ai * hi + hs[pl.ds(row, 8), cre]
                ni = ar * hi + ai * hr + hs[pl.ds(row, 8), cim]
                hs[pl.ds(row, 8), cre] = nr
                hs[pl.ds(row, 8), cim] = ni
                return nr, ni

            hr, hi = lax.fori_loop(0, tl, step, (st[pl.ds(b0, 8), cre], st[pl.ds(b0, 8), cim]),
                                   unroll=min(tl, 4))
            st[pl.ds(b0, 8), cre] = hr
            st[pl.ds(b0, 8), cim] = hi
            return carry

        lax.fori_loop(0, nb // 8, seq_group, 0)


def _s5_core(ut, ut_b, bblk_ref, are_ref, aim_ref, cblk_ref, d_ref, hs, st, nb, tl):
    for k in range(S5_CHUNKS):
        hs[:, 2 * k * S5_CW:(2 * k + 2) * S5_CW] = _dot(ut_b[:, k * 256:(k + 1) * 256], bblk_ref[k])
    _s5_scan(hs, st, are_ref, aim_ref, nb, tl)
    ys = [_dot(hs[:, 2 * k * S5_CW:(2 * k + 2) * S5_CW].astype(BF16), cblk_ref[k]) for k in range(S5_CHUNKS)]
    y = jnp.concatenate(ys, axis=-1) + d_ref[...] * ut
    return jax.nn.gelu(y, approximate=True).astype(BF16)


def _s5_state_in(st, h0re_ref, h0im_ref):
    for k in range(S5_CHUNKS):
        st[:, 2 * k * S5_CW:(2 * k + 1) * S5_CW] = h0re_ref[:, k * S5_CW:(k + 1) * S5_CW]
        st[:, (2 * k + 1) * S5_CW:(2 * k + 2) * S5_CW] = h0im_ref[:, k * S5_CW:(k + 1) * S5_CW]


def _s5_state_out(st, hre_ref, him_ref):
    for k in range(S5_CHUNKS):
        hre_ref[:, k * S5_CW:(k + 1) * S5_CW] = st[:, 2 * k * S5_CW:(2 * k + 1) * S5_CW]
        him_ref[:, k * S5_CW:(k + 1) * S5_CW] = st[:, (2 * k + 1) * S5_CW:(2 * k + 2) * S5_CW]


def _glu_out(z):
    return z[:, :D] * jax.nn.sigmoid(z[:, D:])


def _s5_param_specs():
    return [_const_spec((DEPTH, D)),
            _const_spec((S5_CHUNKS, 256, 2 * S5_CW)),
            _const_spec((8, SSM_LANES)),
            _const_spec((8, SSM_LANES)),
            _const_spec((S5_CHUNKS, 2 * S5_CW, 256)),
            _const_spec((1, D)),
            _layer_spec((D, 2 * D), 0)]


def _row_permutation(nb, tl, transpose):
    rows = nb * tl
    i0 = lax.broadcasted_iota(jnp.int32, (rows, rows), 0)
    i1 = lax.broadcasted_iota(jnp.int32, (rows, rows), 1)
    dst, src = (i1, i0) if transpose else (i0, i1)
    lg_nb, lg_tl = nb.bit_length() - 1, tl.bit_length() - 1
    hit = ((dst >> lg_nb) == (src & (tl - 1))) & ((dst & (nb - 1)) == (src >> lg_tl))
    return jnp.where(hit, 1.0, 0.0).astype(BF16)


def _s5_prompt_kernel(x_ref, g_ref, bblk_ref, are_ref, aim_ref, cblk_ref, d_ref, wglu_ref,
                      h0re_ref, h0im_ref, o_ref, hre_ref, him_ref, hs, st, perm_ref, permt_ref, *, nb, tl):
    t = pl.program_id(0)

    @pl.when(t == 0)
    def _():
        _s5_state_in(st, h0re_ref, h0im_ref)
        perm_ref[...] = _row_permutation(nb, tl, False)
        permt_ref[...] = _row_permutation(nb, tl, True)

    x = x_ref[...].reshape(nb * tl, D)
    u = _rms(x, g_ref[1:2, :])
    u_hi = u.astype(BF16)
    u_lo = (u - u_hi.astype(F32)).astype(BF16)
    ut_hi = _dot(perm_ref[...], u_hi)
    ut = ut_hi + _dot(perm_ref[...], u_lo)
    ge = _s5_core(ut, ut_hi.astype(BF16), bblk_ref, are_ref, aim_ref, cblk_ref, d_ref, hs, st, nb, tl)
    ge = _dot(permt_ref[...], ge).astype(BF16)
    out = _glu_out(_dot(ge, wglu_ref[...]))
    o_ref[...] = (x + out).reshape(nb, tl, D)

    @pl.when(t == pl.num_programs(0) - 1)
    def _():
        _s5_state_out(st, hre_ref, him_ref)


def _s5_prompt(x3d, s5_p, h0re, h0im, *, tl):
    nb, l, _ = x3d.shape
    assert nb % 8 == 0 and nb & (nb - 1) == 0 and tl & (tl - 1) == 0
    rows = nb * tl
    kern = functools.partial(_s5_prompt_kernel, nb=nb, tl=tl)
    st_shape = jax.ShapeDtypeStruct((nb, SSM_LANES), F32)
    return pl.pallas_call(
        kern,
        grid=(l // tl,),
        in_specs=[pl.BlockSpec((nb, tl, D), lambda t: (0, t, 0))] + _s5_param_specs() + [
                  _const_spec((nb, SSM_LANES)),
                  _const_spec((nb, SSM_LANES))],
        out_specs=[pl.BlockSpec((nb, tl, D), lambda t: (0, t, 0)),
                   pl.BlockSpec((nb, SSM_LANES), lambda t: (0, 0)),
                   pl.BlockSpec((nb, SSM_LANES), lambda t: (0, 0))],
        out_shape=[jax.ShapeDtypeStruct(x3d.shape, F32), st_shape, st_shape],
        scratch_shapes=[pltpu.VMEM((rows, 2 * SSM_LANES), F32), pltpu.VMEM((nb, 2 * SSM_LANES), F32),
                        pltpu.VMEM((rows, rows), BF16), pltpu.VMEM((rows, rows), BF16)],
        compiler_params=_params(("arbitrary",), 48),
        name="s5_prompt",
    )(x3d, *s5_p, h0re, h0im)


def _s5_sample_kernel(x_ref, g_ref, bblk_ref, are_ref, aim_ref, cblk_ref, d_ref, wglu_ref,
                      h0re_ref, h0im_ref, o_ref, hre_ref, him_ref, hs, st, *, nb, seq_len):
    _s5_state_in(st, h0re_ref, h0im_ref)
    xt = jnp.concatenate([x_ref[:, t * D:(t + 1) * D] for t in range(seq_len)], axis=0)
    ut = _rms(xt, g_ref[1:2, :])
    ge = _s5_core(ut, ut.astype(BF16), bblk_ref, are_ref, aim_ref, cblk_ref, d_ref, hs, st, nb, seq_len)
    out = _glu_out(_dot(ge, wglu_ref[...]))
    for t in range(seq_len):
        o_ref[:, t * D:(t + 1) * D] = x_ref[:, t * D:(t + 1) * D] + out[t * nb:(t + 1) * nb, :]
    _s5_state_out(st, hre_ref, him_ref)


def _s5_sample(xv, s5_p, h0re, h0im, *, nb, seq_len):
    nseq = xv.shape[0]
    assert nseq % nb == 0 and nb % 8 == 0
    rows = nb * seq_len
    kern = functools.partial(_s5_sample_kernel, nb=nb, seq_len=seq_len)
    st_shape = jax.ShapeDtypeStruct((nseq, SSM_LANES), F32)
    st_spec = pl.BlockSpec((nb, SSM_LANES), lambda i: (i, 0))
    return pl.pallas_call(
        kern,
        grid=(nseq // nb,),
        in_specs=[pl.BlockSpec((nb, seq_len * D), lambda i: (i, 0))] + _s5_param_specs() + [st_spec, st_spec],
        out_specs=[pl.BlockSpec((nb, seq_len * D), lambda i: (i, 0)), st_spec, st_spec],
        out_shape=[jax.ShapeDtypeStruct(xv.shape, F32), st_shape, st_shape],
        scratch_shapes=[pltpu.VMEM((rows, 2 * SSM_LANES), F32), pltpu.VMEM((nb, 2 * SSM_LANES), F32)],
        compiler_params=_params(("arbitrary",), 48),
        name="s5_sample",
    )(xv, *s5_p, h0re, h0im)


def _s5_discretize(a_re, a_im, log_dt, b_re, b_im, c_re, c_im):
    dt = jnp.exp(log_dt)[:, None]
    mag = jnp.exp(dt * a_re)
    ab_re = mag * jnp.cos(dt * a_im)
    ab_im = mag * jnp.sin(dt * a_im)
    den = a_re * a_re + a_im * a_im
    num_re = ab_re - 1.0
    coef_re = (num_re * a_re + ab_im * a_im) / den
    coef_im = (ab_im * a_re - num_re * a_im) / den
    bb_re = coef_re[..., None] * b_re - coef_im[..., None] * b_im
    bb_im = coef_re[..., None] * b_im + coef_im[..., None] * b_re
    gpc = SSM_GROUPS // S5_CHUNKS
    diag = jnp.eye(gpc, dtype=F32)
    bb = jnp.stack([bb_re, bb_im]).reshape(2, S5_CHUNKS, gpc, SSM_STATE, SSM_GROUP)
    bb = jnp.transpose(bb, (1, 4, 0, 2, 3))
    bblk = bb[:, None] * diag[None, :, None, None, :, None]
    bblk = bblk.astype(BF16).reshape(S5_CHUNKS, gpc * SSM_GROUP, 2 * S5_CW)
    cc = jnp.stack([c_re, -c_im]).reshape(2, S5_CHUNKS, gpc, SSM_GROUP, SSM_STATE)
    cc = jnp.transpose(cc, (1, 0, 2, 4, 3))
    cblk = cc[:, :, :, :, None, :] * diag[None, None, :, None, :, None]
    cblk = cblk.astype(BF16).reshape(S5_CHUNKS, 2 * S5_CW, gpc * SSM_GROUP)
    are = jnp.broadcast_to(ab_re.reshape(1, SSM_LANES), (8, SSM_LANES))
    aim = jnp.broadcast_to(ab_im.reshape(1, SSM_LANES), (8, SSM_LANES))
    return bblk, cblk, are, aim


S5_TL = 32
CONV_TL = 256
ATTN_TQ = 512
MLP_TM = 512


def kernel(x_prompt, x_sample, mem_prompt, cache_conv, state_s5_re, state_s5_im, cache_mem_k, cache_mem_v,
           norm_mix, norm_xattn, norm_ffn, norm_final,
           conv_w_in, conv_b_in, conv_dw, conv_dw_b, conv_ln_g, conv_ln_b, conv_w_out,
           s5_a_re, s5_a_im, s5_log_dt, s5_b_re, s5_b_im, s5_c_re, s5_c_im, s5_d, s5_w_glu,
           xattn_w_q, xattn_w_k, xattn_w_v, xattn_w_o, mlp_w_up, mlp_w_down):
    bp, lp, _ = x_prompt.shape
    bs, ls, _ = x_sample.shape

    wq, wk, wv, wo = (w.astype(BF16) for w in (xattn_w_q, xattn_w_k, xattn_w_v, xattn_w_o))
    wup, wdn = mlp_w_up.astype(BF16), mlp_w_down.astype(BF16)
    conv_p = (norm_mix, conv_w_in.astype(BF16), conv_b_in, conv_dw, conv_dw_b, conv_ln_g, conv_ln_b,
              conv_w_out.astype(BF16))
    bblk, cblk, are, aim = _s5_discretize(s5_a_re[0], s5_a_im[0], s5_log_dt[0], s5_b_re[0], s5_b_im[0],
                                          s5_c_re[0], s5_c_im[0])
    s5_p = (norm_mix, bblk, are, aim, cblk, s5_d, s5_w_glu.astype(BF16))
    gfin = norm_final.reshape(1, D)

    kp, vp = _mem_kv(mem_prompt.reshape(bp * N_MEM, D), wk, wv)
    kp4 = kp.reshape(DEPTH, bp, KV_ROWS, LANES)
    vp4 = vp.reshape(DEPTH, bp, KV_ROWS, LANES)
    mem_k_prompt = _kv_rows_unview(kp4)
    mem_v_prompt = _kv_rows_unview(vp4)

    def mlp(x2d, i, hid_steps, final):
        return _mlp(x2d, norm_ffn, wup, wdn, gfin, i, tm=MLP_TM, hid_steps=hid_steps, final_norm=final)

    x, hist = _conv_prompt(x_prompt, conv_p, tl=CONV_TL)
    conv_prompt = hist[None, :, HIST_PAD - CONV_BUF:, :]
    x = _attn_prompt(x, norm_xattn, wq, kp4, vp4, wo, 0, tq=ATTN_TQ)
    x = mlp(x.reshape(bp * lp, D), 0, 1, False).reshape(bp, lp, D)
    zeros_state = jnp.zeros((bp, SSM_LANES), F32)
    x, hre, him = _s5_prompt(x, s5_p, zeros_state, zeros_state, tl=S5_TL)
    s5_re_prompt = hre.reshape(1, bp, SSM_GROUPS, SSM_STATE)
    s5_im_prompt = him.reshape(1, bp, SSM_GROUPS, SSM_STATE)
    x = _attn_prompt(x, norm_xattn, wq, kp4, vp4, wo, 1, tq=ATTN_TQ)
    y_prompt = mlp(x.reshape(bp * lp, D), 1, 1, True).reshape(bp, lp, D)

    ck = _kv_rows_view(cache_mem_k)
    cv = _kv_rows_view(cache_mem_v)
    hist3 = jnp.transpose(cache_conv[0], (1, 0, 2))
    xv, nhist = _conv_sample(x_sample.reshape(bs, ls * D), hist3, conv_p, nb=32, seq_len=ls)
    conv_sample = jnp.transpose(nhist, (1, 0, 2))[None]
    x = _attn_sample(xv.reshape(bs * ls, D), norm_xattn, wq, ck, cv, wo, 0, sb=4, seq_len=ls)
    x = mlp(x, 0, 4, False)
    xv, hre, him = _s5_sample(x.reshape(bs, ls * D), s5_p,
                              state_s5_re[0].reshape(bs, SSM_LANES), state_s5_im[0].reshape(bs, SSM_LANES),
                              nb=64, seq_len=ls)
    s5_re_sample = hre.reshape(1, bs, SSM_GROUPS, SSM_STATE)
    s5_im_sample = him.reshape(1, bs, SSM_GROUPS, SSM_STATE)
    x = _attn_sample(xv.reshape(bs * ls, D), norm_xattn, wq, ck, cv, wo, 1, sb=4, seq_len=ls)
    y_sample = mlp(x, 1, 4, True).reshape(bs, ls, D)

    return (y_prompt, y_sample, conv_prompt, conv_sample, s5_re_prompt, s5_im_prompt,
            s5_re_sample, s5_im_sample, mem_k_prompt, mem_v_prompt)
```

```python
import functools

import jax
import jax.numpy as jnp
from jax import lax
from jax.experimental import pallas as pl
from jax.experimental.pallas import tpu as pltpu

F32 = jnp.float32
BF16 = jnp.bfloat16

LANES = 128
D = 1024
DEPTH = 2
D_FF = 4 * D
N_MEM = 256
HEADS = 4
HD = D // HEADS
HD_TILES = HD // LANES
KV_ROWS = N_MEM * HEADS * HD_TILES
CONV_W = 31
CONV_BUF = CONV_W - 1
HIST_PAD = 32
SSM_GROUP = 16
SSM_GROUPS = D // SSM_GROUP
SSM_STATE = 64
SSM_LANES = SSM_GROUPS * SSM_STATE
S5_CHUNKS = 4
S5_CW = SSM_LANES // S5_CHUNKS
EPS = 1e-6
ATTN_SCALE = HD ** -0.5
NT_DIMS = (((1,), (1,)), ((), ()))
MIB = 2 ** 20


def _dot(a, b):
    return jnp.dot(a, b, preferred_element_type=F32)


def _rms(x, g):
    ms = jnp.mean(x * x, axis=-1, keepdims=True)
    return x * lax.rsqrt(ms + EPS) * g


def _softmax(s):
    m = jnp.max(s, axis=-1, keepdims=True)
    e = jnp.exp(s - m)
    return e / jnp.sum(e, axis=-1, keepdims=True)


def _const_spec(shape):
    nd = len(shape)
    return pl.BlockSpec(shape, lambda *_: (0,) * nd, pipeline_mode=pl.Buffered(1))


def _layer_spec(shape, layer):
    nd = len(shape)
    return pl.BlockSpec((None,) + shape, lambda *_: (layer,) + (0,) * nd, pipeline_mode=pl.Buffered(1))


def _params(sem, vmem_mib):
    return pltpu.CompilerParams(dimension_semantics=sem, vmem_limit_bytes=vmem_mib * MIB)


def _kv_rows_view(kv5d):
    lead = kv5d.shape[:-3]
    v = kv5d.reshape(lead + (N_MEM, HEADS, HD_TILES, LANES))
    v = jnp.swapaxes(v, -3, -2)
    return v.reshape(lead + (KV_ROWS, LANES))


def _kv_rows_unview(rows4d):
    lead = rows4d.shape[:-2]
    v = rows4d.reshape(lead + (N_MEM, HD_TILES, HEADS, LANES))
    v = jnp.swapaxes(v, -3, -2)
    return v.reshape(lead + (N_MEM, HEADS, HD))


def _head_rows(half, h, n):
    return pl.ds(half * HEADS + h, n, stride=HEADS * HD_TILES)


def _load_head(ref, lead, h):
    parts = [ref[lead + (_head_rows(e, h, N_MEM), slice(None))] for e in range(HD_TILES)]
    return jnp.concatenate(parts, axis=1).astype(BF16)


def _kv_kernel(mem_ref, wk_ref, wv_ref, k_ref, v_ref, wk_sc, wv_sc, *, tm):
    @pl.when(pl.program_id(1) == 0)
    def _():
        wk_sc[...] = wk_ref[...].astype(BF16)
        wv_sc[...] = wv_ref[...].astype(BF16)

    m = mem_ref[...].astype(BF16)
    for w_sc, o_ref in ((wk_sc, k_ref), (wv_sc, v_ref)):
        y = _dot(m, w_sc[...])
        for h in range(HEADS):
            for e in range(HD_TILES):
                c0 = h * HD + e * LANES
                o_ref[0, _head_rows(e, h, tm), :] = y[:, c0:c0 + LANES]


def _mem_kv(mem2d, wk, wv):
    rows = mem2d.shape[0]
    tm = 512
    rpm = HEADS * HD_TILES
    out = jax.ShapeDtypeStruct((DEPTH, rows * rpm, LANES), F32)
    return pl.pallas_call(
        functools.partial(_kv_kernel, tm=tm),
        grid=(DEPTH, rows // tm),
        in_specs=[pl.BlockSpec((tm, D), lambda l, r: (r, 0)),
                  pl.BlockSpec((None, D, D), lambda l, r: (l, 0, 0)),
                  pl.BlockSpec((None, D, D), lambda l, r: (l, 0, 0))],
        out_specs=[pl.BlockSpec((1, tm * rpm, LANES), lambda l, r: (l, r, 0)),
                   pl.BlockSpec((1, tm * rpm, LANES), lambda l, r: (l, r, 0))],
        out_shape=[out, out],
        scratch_shapes=[pltpu.VMEM((D, D), BF16), pltpu.VMEM((D, D), BF16)],
        compiler_params=_params(("arbitrary", "arbitrary"), 40),
        name="mem_kv",
    )(mem2d, wk, wv)


MLP_CH = 1024


def _mlp_kernel(x_ref, g_ref, wup_ref, wdn_ref, gfin_ref, *rest, layer, n_inner, final_norm, emit_bf16):
    if emit_bf16:
        o_ref, wup_b, wdn_b, xn_sc, acc_sc = rest
        wup_b[...] = wup_ref[...].astype(BF16)
        wdn_b[...] = wdn_ref[...].astype(BF16)
    else:
        o_ref, xn_sc, acc_sc = rest
        wup_b, wdn_b = wup_ref, wdn_ref
    c = pl.program_id(1)

    @pl.when(c == 0)
    def _():
        x = x_ref[...]
        xn_sc[...] = _rms(x, g_ref[layer:layer + 1, :]).astype(BF16)
        acc_sc[...] = x

    xn = xn_sc[...]
    acc = acc_sc[...]
    for j in range(n_inner):
        h = _dot(xn, wup_b[:, j * MLP_CH:(j + 1) * MLP_CH])
        h = jnp.maximum(h, 0.0)
        acc = acc + _dot((h * h).astype(BF16), wdn_b[j * MLP_CH:(j + 1) * MLP_CH, :])
    acc_sc[...] = acc

    @pl.when(c == pl.num_programs(1) - 1)
    def _():
        if final_norm:
            o_ref[...] = _rms(acc, gfin_ref[...])
        else:
            o_ref[...] = acc


def _mlp_call(x2d, g, wup, wdn, gfin, layer, w_specs, extra_out_specs, extra_out_shapes, *, tm, hid_steps,
              final_norm):
    rows = x2d.shape[0]
    kern = functools.partial(_mlp_kernel, layer=layer, n_inner=D_FF // hid_steps // MLP_CH, final_norm=final_norm,
                             emit_bf16=bool(extra_out_specs))
    return pl.pallas_call(
        kern,
        grid=(rows // tm, hid_steps),
        in_specs=[pl.BlockSpec((tm, D), lambda r, c: (r, 0)), _const_spec((DEPTH, D))] + w_specs + [
                  _const_spec((1, D))],
        out_specs=[pl.BlockSpec((tm, D), lambda r, c: (r, 0))] + extra_out_specs,
        out_shape=[jax.ShapeDtypeStruct((rows, D), F32)] + extra_out_shapes,
        scratch_shapes=[pltpu.VMEM((tm, D), BF16), pltpu.VMEM((tm, D), F32)],
        compiler_params=_params(("arbitrary", "arbitrary"), 48),
        name="mlp",
    )(x2d, g, wup, wdn, gfin)


def _mlp_resident(x2d, g, wup_b, wdn_b, gfin, layer, *, tm, final_norm):
    w_specs = [_const_spec((D, D_FF)), _const_spec((D_FF, D))]
    return _mlp_call(x2d, g, wup_b, wdn_b, gfin, layer, w_specs, [], [], tm=tm, hid_steps=1,
                     final_norm=final_norm)[0]


def _mlp_streaming(x2d, g, wup, wdn, gfin, layer, *, tm, hid_steps, final_norm):
    hid_blk = D_FF // hid_steps
    w_specs = [pl.BlockSpec((None, D, hid_blk), lambda r, c: (layer, 0, c)),
               pl.BlockSpec((None, hid_blk, D), lambda r, c: (layer, c, 0))]
    b_specs = [pl.BlockSpec((D, hid_blk), lambda r, c: (0, c)), pl.BlockSpec((hid_blk, D), lambda r, c: (c, 0))]
    b_shapes = [jax.ShapeDtypeStruct((D, D_FF), BF16), jax.ShapeDtypeStruct((D_FF, D), BF16)]
    return _mlp_call(x2d, g, wup, wdn, gfin, layer, w_specs, b_specs, b_shapes, tm=tm, hid_steps=hid_steps,
                     final_norm=final_norm)


def _attn_prompt_kernel(x_ref, g_ref, wq_ref, k_ref, v_ref, wo_ref, o_ref, *, layer, tq, sub):
    starts = range(0, tq, sub)
    xs = [x_ref[0, r0:r0 + sub, :] for r0 in starts]
    qs = [(_dot(_rms(x, g_ref[layer:layer + 1, :]).astype(BF16), wq_ref[...]) * ATTN_SCALE).astype(BF16) for x in xs]
    outs = [[] for _ in starts]
    for h in range(HEADS):
        cols = slice(h * HD, (h + 1) * HD)
        kh = _load_head(k_ref, (), h)
        vh = _load_head(v_ref, (), h)
        for q, out in zip(qs, outs):
            s = lax.dot_general(q[:, cols], kh, NT_DIMS, preferred_element_type=F32)
            out.append(_dot(_softmax(s).astype(BF16), vh).astype(BF16))
    for r0, x, out in zip(starts, xs, outs):
        o_ref[0, r0:r0 + sub, :] = x + _dot(jnp.concatenate(out, axis=-1), wo_ref[...])


def _attn_prompt(x3d, g, wq, k4d, v4d, wo, layer, *, tq):
    b, l, _ = x3d.shape
    kv_spec = pl.BlockSpec((None, None, KV_ROWS, LANES), lambda i, t: (layer, i, 0, 0))
    return pl.pallas_call(
        functools.partial(_attn_prompt_kernel, layer=layer, tq=tq, sub=ATTN_SUB),
        grid=(b, l // tq),
        in_specs=[pl.BlockSpec((1, tq, D), lambda i, t: (i, t, 0)),
                  _const_spec((DEPTH, D)),
                  _const_spec((D, D)),
                  kv_spec, kv_spec,
                  _const_spec((D, D))],
        out_specs=pl.BlockSpec((1, tq, D), lambda i, t: (i, t, 0)),
        out_shape=jax.ShapeDtypeStruct(x3d.shape, F32),
        compiler_params=_params(("arbitrary", "arbitrary"), 40),
        name="attn_prompt",
    )(x3d, g, wq, k4d, v4d, wo)


def _attn_sample_kernel(x_ref, g_ref, wq_ref, k_ref, v_ref, wo_ref, o_ref, wq_b, wo_b, q_sc, o_sc, *,
                        layer, sb, seq_len):
    i = pl.program_id(0)

    @pl.when(i == 0)
    def _():
        wq_b[...] = wq_ref[...].astype(BF16)
        wo_b[...] = wo_ref[...].astype(BF16)
        xn = _rms(x_ref[...], g_ref[layer:layer + 1, :]).astype(BF16)
        q_sc[...] = _dot(xn, wq_b[...]) * ATTN_SCALE

    pair_rows = 2 * seq_len
    first_seq_row = lax.broadcasted_iota(jnp.int32, (pair_rows, HD), 0) < seq_len

    for j in range(sb // 2):
        r0 = pl.multiple_of((i * sb + 2 * j) * seq_len, pair_rows)
        q8 = q_sc[pl.ds(r0, pair_rows), :].astype(BF16)
        scores = [lax.dot_general(q8[:, h * HD:(h + 1) * HD], _load_head(k_ref, (2 * j + s,), h), NT_DIMS,
                                  preferred_element_type=F32)
                  for s in range(2) for h in range(HEADS)]
        p = _softmax(jnp.concatenate(scores, axis=0))
        res = []
        for h in range(HEADS):
            oh = []
            for s in range(2):
                blk = (s * HEADS + h) * pair_rows
                oh.append(_dot(p[blk:blk + pair_rows, :].astype(BF16), _load_head(v_ref, (2 * j + s,), h)))
            res.append(jnp.where(first_seq_row, oh[0], oh[1]))
        o_sc[pl.ds(r0, pair_rows), :] = jnp.concatenate(res, axis=-1)

    @pl.when(i == pl.num_programs(0) - 1)
    def _():
        o_ref[...] = x_ref[...] + _dot(o_sc[...].astype(BF16), wo_b[...])


def _attn_sample(x2d, g, wq, k4d, v4d, wo, layer, *, sb, seq_len):
    rows = x2d.shape[0]
    nseq = rows // seq_len
    assert 2 * seq_len == 8 and nseq % sb == 0 and sb % 2 == 0
    kern = functools.partial(_attn_sample_kernel, layer=layer, sb=sb, seq_len=seq_len)
    kv_spec = pl.BlockSpec((None, sb, KV_ROWS, LANES), lambda i: (layer, i, 0, 0))
    return pl.pallas_call(
        kern,
        grid=(nseq // sb,),
        in_specs=[_const_spec((rows, D)),
                  _const_spec((DEPTH, D)),
                  _layer_spec((D, D), layer),
                  kv_spec, kv_spec,
                  _layer_spec((D, D), layer)],
        out_specs=[pl.BlockSpec((rows, D), lambda i: (0, 0)),
                   pl.BlockSpec((D, D), lambda i: (0, 0)),
                   pl.BlockSpec((D, D), lambda i: (0, 0))],
        out_shape=[jax.ShapeDtypeStruct((rows, D), F32),
                   jax.ShapeDtypeStruct((D, D), BF16),
                   jax.ShapeDtypeStruct((D, D), BF16)],
        scratch_shapes=[pltpu.VMEM((rows, D), F32), pltpu.VMEM((rows, D), F32)],
        compiler_params=_params(("arbitrary",), 56),
        name="attn_sample",
    )(x2d, g, wq, k4d, v4d, wo)


def _glu_in(x, g, win_ref, bin_ref):
    xn = _rms(x, g).astype(BF16)
    h = _dot(xn, win_ref[...]) + bin_ref[...]
    return h[:, :D] * jax.nn.sigmoid(h[:, D:])


def _ln_silu_out(c, lng_ref, lnb_ref, wout_ref):
    mu = jnp.mean(c, axis=-1, keepdims=True)
    cc = c - mu
    var = jnp.mean(cc * cc, axis=-1, keepdims=True)
    n = cc * lax.rsqrt(var + EPS) * lng_ref[...] + lnb_ref[...]
    s = (n * jax.nn.sigmoid(n)).astype(BF16)
    return _dot(s, wout_ref[...])


def _conv_prompt_kernel(x_ref, g_ref, win_ref, bin_ref, dw_ref, dwb_ref, lng_ref, lnb_ref, wout_ref,
                        o_ref, hist_ref, gbuf, cbuf, *, tl, ns):
    n_tiles = D // LANES
    n_groups = tl // 8
    base = HIST_PAD - CONV_BUF

    def rows2(start, n):
        return pl.ds(2 * start, n, stride=2)

    @pl.when(pl.program_id(1) == 0)
    def _():
        for s in range(ns):
            for j in range(n_tiles):
                gbuf[s, j, rows2(0, HIST_PAD), :] = jnp.zeros((HIST_PAD, LANES), F32)

    xs = [x_ref[s] for s in range(ns)]
    xns = [_rms(x, g_ref[0:1, :]).astype(BF16) for x in xs]

    def glu_block(s, c):
        va = slice(c * 256, (c + 1) * 256)
        ga = slice(D + c * 256, D + (c + 1) * 256)
        val = _dot(xns[s], win_ref[:, va]) + bin_ref[:, va]
        gate = _dot(xns[s], win_ref[:, ga]) + bin_ref[:, ga]
        gl = val * jax.nn.sigmoid(gate)
        for jj in range(256 // LANES):
            gbuf[s, 2 * c + jj, rows2(HIST_PAD, tl), :] = gl[:, jj * LANES:(jj + 1) * LANES]

    def conv_tile(s, j):
        lanes = slice(j * LANES, (j + 1) * LANES)
        w = [jnp.broadcast_to(dw_ref[k:k + 1, lanes], (8, LANES)) for k in range(CONV_W)]
        bias = jnp.broadcast_to(dwb_ref[:, lanes], (8, LANES))
        accs = [bias] * 4
        for i in range(n_groups + 3):
            lags = range(max(0, i - n_groups + 1), min(i, 3) + 1)
            for r in range(8):
                taps = [(a, 8 * a + r) for a in lags if 8 * a + r < CONV_W]
                if taps:
                    wnd = gbuf[s, j, rows2(base + 8 * i + r, 8), :]
                    for a, k in taps:
                        accs[a] = accs[a] + w[k] * wnd
            if i >= 3:
                cbuf[s, (i - 3) * 8:(i - 2) * 8, lanes] = accs[3]
            accs = [bias, accs[0], accs[1], accs[2]]

    def finish(s):
        o_ref[s] = xs[s] + _ln_silu_out(cbuf[s], lng_ref, lnb_ref, wout_ref)
        for j in range(n_tiles):
            tail = gbuf[s, j, rows2(tl, HIST_PAD), :]
            hist_ref[s, :, j * LANES:(j + 1) * LANES] = tail
            gbuf[s, j, rows2(0, HIST_PAD), :] = tail

    for c in range(D // 256):
        glu_block(0, c)
    for s in range(ns):
        for j in range(n_tiles):
            conv_tile(s, j)
            if s + 1 < ns and j % 2 == 1:
                glu_block(s + 1, j // 2)
        finish(s)


def _conv_param_specs():
    return [_const_spec((DEPTH, D)),
            _layer_spec((D, 2 * D), 0),
            _const_spec((1, 2 * D)),
            _layer_spec((CONV_W, D), 0),
            _const_spec((1, D)),
            _const_spec((1, D)),
            _const_spec((1, D)),
            _layer_spec((D, D), 0)]


def _conv_prompt(x3d, conv_p, *, tl):
    b, l, _ = x3d.shape
    ns = CONV_NS
    assert b % ns == 0
    kern = functools.partial(_conv_prompt_kernel, tl=tl, ns=ns)
    return pl.pallas_call(
        kern,
        grid=(b // ns, l // tl),
        in_specs=[pl.BlockSpec((ns, tl, D), lambda i, t: (i, t, 0))] + _conv_param_specs(),
        out_specs=[pl.BlockSpec((ns, tl, D), lambda i, t: (i, t, 0)),
                   pl.BlockSpec((ns, HIST_PAD, D), lambda i, t: (i, 0, 0))],
        out_shape=[jax.ShapeDtypeStruct(x3d.shape, F32),
                   jax.ShapeDtypeStruct((b, HIST_PAD, D), F32)],
        scratch_shapes=[pltpu.VMEM((ns, D // LANES, 2 * (HIST_PAD + tl), LANES), F32),
                        pltpu.VMEM((ns, tl, D), F32)],
        compiler_params=_params(("arbitrary", "arbitrary"), 40),
        name="conv_prompt",
    )(x3d, *conv_p)


def _conv_sample_kernel(x_ref, hist_ref, g_ref, win_ref, bin_ref, dw_ref, dwb_ref, lng_ref, lnb_ref, wout_ref,
                        o_ref, nhist_ref, *, nb, seq_len):
    xt = jnp.concatenate([x_ref[:, t * D:(t + 1) * D] for t in range(seq_len)], axis=0)
    gl = _glu_in(xt, g_ref[0:1, :], win_ref, bin_ref)
    g_t = [gl[t * nb:(t + 1) * nb, :] for t in range(seq_len)]

    def padded(r, lanes):
        if r < CONV_BUF:
            return hist_ref[r, :, lanes]
        return g_t[r - CONV_BUF][:, lanes]

    cs = []
    for t in range(seq_len):
        tiles = []
        for j in range(D // LANES):
            lanes = slice(j * LANES, (j + 1) * LANES)
            acc = None
            for k in range(CONV_W):
                term = dw_ref[k:k + 1, lanes] * padded(t + k, lanes)
                acc = term if acc is None else acc + term
            tiles.append(acc)
        cs.append(jnp.concatenate(tiles, axis=-1))
    c = jnp.concatenate(cs, axis=0) + dwb_ref[...]
    res = _ln_silu_out(c, lng_ref, lnb_ref, wout_ref)
    for t in range(seq_len):
        o_ref[:, t * D:(t + 1) * D] = x_ref[:, t * D:(t + 1) * D] + res[t * nb:(t + 1) * nb, :]
    keep = CONV_BUF - seq_len
    nhist_ref[0:keep] = hist_ref[seq_len:CONV_BUF]
    for t in range(seq_len):
        nhist_ref[keep + t] = g_t[t]


def _conv_sample(xv, hist3, conv_p, *, nb, seq_len):
    nseq = xv.shape[0]
    assert seq_len <= CONV_BUF and nseq % nb == 0 and nb % 8 == 0
    kern = functools.partial(_conv_sample_kernel, nb=nb, seq_len=seq_len)
    hist_spec = pl.BlockSpec((CONV_BUF, nb, D), lambda i: (0, i, 0))
    return pl.pallas_call(
        kern,
        grid=(nseq // nb,),
        in_specs=[pl.BlockSpec((nb, seq_len * D), lambda i: (i, 0)), hist_spec] + _conv_param_specs(),
        out_specs=[pl.BlockSpec((nb, seq_len * D), lambda i: (i, 0)), hist_spec],
        out_shape=[jax.ShapeDtypeStruct(xv.shape, F32),
                   jax.ShapeDtypeStruct(hist3.shape, F32)],
        compiler_params=_params(("arbitrary",), 40),
        name="conv_sample",
    )(xv, hist3, *conv_p)


def _s5_scan_chunk_static(hs, st, are_ref, aim_ref, k, tl):
    ar = are_ref[:, k * S5_CW:(k + 1) * S5_CW]
    ai = aim_ref[:, k * S5_CW:(k + 1) * S5_CW]
    cre = slice(2 * k * S5_CW, (2 * k + 1) * S5_CW)
    cim = slice((2 * k + 1) * S5_CW, (2 * k + 2) * S5_CW)
    hr, hi = st[:, cre], st[:, cim]
    for t in range(tl):
        rows = slice(t * 8, (t + 1) * 8)
        hr, hi = ar * hr - ai * hi + hs[rows, cre], ar * hi + ai * hr + hs[rows, cim]
        hs[rows, cre] = hr
        hs[rows, cim] = hi
    st[:, cre] = hr
    st[:, cim] = hi


def _s5_scan(hs, st, are_ref, aim_ref, nb, tl):
    for k in range(S5_CHUNKS):
        ar = are_ref[:, k * S5_CW:(k + 1) * S5_CW]
        ai = aim_ref[:, k * S5_CW:(k + 1) * S5_CW]
        cre = slice(2 * k * S5_CW, (2 * k + 1) * S5_CW)
        cim = slice((2 * k + 1) * S5_CW, (2 * k + 2) * S5_CW)

        def seq_group(bg, carry, ar=ar, ai=ai, cre=cre, cim=cim):
            b0 = pl.multiple_of(bg * 8, 8)

            def step(t, h):
                hr, hi = h
                row = pl.multiple_of(t * nb + b0, 8)
                nr = ar * hr - ai * hi + hs[pl.ds(row, 8), cre]
                ni = ar * hi + ai * hr + hs[pl.ds(row, 8), cim]
                hs[pl.ds(row, 8), cre] = nr
                hs[pl.ds(row, 8), cim] = ni
                return nr, ni

            hr, hi = lax.fori_loop(0, tl, step, (st[pl.ds(b0, 8), cre], st[pl.ds(b0, 8), cim]),
                                   unroll=min(tl, 4))
            st[pl.ds(b0, 8), cre] = hr
            st[pl.ds(b0, 8), cim] = hi
            return carry

        lax.fori_loop(0, nb // 8, seq_group, 0)


def _s5_core(ut, ut_b, bblk_ref, are_ref, aim_ref, cblk_ref, d_ref, hs, st, nb, tl):
    for k in range(S5_CHUNKS):
        hs[:, 2 * k * S5_CW:(2 * k + 2) * S5_CW] = _dot(ut_b[:, k * 256:(k + 1) * 256], bblk_ref[k])
    if nb == 8:
        for k in range(S5_CHUNKS):
            _s5_scan_chunk_static(hs, st, are_ref, aim_ref, k, tl)
    else:
        _s5_scan(hs, st, are_ref, aim_ref, nb, tl)
    ys = [lax.dot_general(hs[:, 2 * k * S5_CW:(2 * k + 2) * S5_CW].astype(BF16), cblk_ref[k], NT_DIMS,
                          preferred_element_type=F32) for k in range(S5_CHUNKS)]
    y = jnp.concatenate(ys, axis=-1) + d_ref[...] * ut
    return jax.nn.gelu(y, approximate=True).astype(BF16)


def _s5_state_in(st, h0re_ref, h0im_ref):
    for k in range(S5_CHUNKS):
        st[:, 2 * k * S5_CW:(2 * k + 1) * S5_CW] = h0re_ref[:, k * S5_CW:(k + 1) * S5_CW]
        st[:, (2 * k + 1) * S5_CW:(2 * k + 2) * S5_CW] = h0im_ref[:, k * S5_CW:(k + 1) * S5_CW]


def _s5_state_out(st, hre_ref, him_ref):
    for k in range(S5_CHUNKS):
        hre_ref[:, k * S5_CW:(k + 1) * S5_CW] = st[:, 2 * k * S5_CW:(2 * k + 1) * S5_CW]
        him_ref[:, k * S5_CW:(k + 1) * S5_CW] = st[:, (2 * k + 1) * S5_CW:(2 * k + 2) * S5_CW]


def _glu_out(z):
    return z[:, :D] * jax.nn.sigmoid(z[:, D:])


def _s5_param_specs():
    return [_const_spec((DEPTH, D)),
            _const_spec((S5_CHUNKS, 256, 2 * S5_CW)),
            _const_spec((8, SSM_LANES)),
            _const_spec((8, SSM_LANES)),
            _const_spec((S5_CHUNKS, 256, 2 * S5_CW)),
            _const_spec((1, D)),
            _layer_spec((D, 2 * D), 0)]


def _row_permutation(nb, tl, transpose):
    rows = nb * tl
    i0 = lax.broadcasted_iota(jnp.int32, (rows, rows), 0)
    i1 = lax.broadcasted_iota(jnp.int32, (rows, rows), 1)
    dst, src = (i1, i0) if transpose else (i0, i1)
    lg_nb, lg_tl = nb.bit_length() - 1, tl.bit_length() - 1
    hit = ((dst >> lg_nb) == (src & (tl - 1))) & ((dst & (nb - 1)) == (src >> lg_tl))
    return jnp.where(hit, 1.0, 0.0).astype(BF16)


def _s5_prompt_kernel(x_ref, g_ref, bblk_ref, are_ref, aim_ref, cblk_ref, d_ref, wglu_ref,
                      h0re_ref, h0im_ref, o_ref, hre_ref, him_ref, hs, st, perm_ref, permt_ref, *,
                      nb, tl, n_sub):
    t = pl.program_id(0)
    rows = nb * tl

    @pl.when(t == 0)
    def _():
        _s5_state_in(st, h0re_ref, h0im_ref)
        perm_ref[...] = _row_permutation(nb, tl, False)
        permt_ref[...] = _row_permutation(nb, tl, True)

    def chunk_cols(k):
        return slice(2 * k * S5_CW, (2 * k + 2) * S5_CW)

    def c_proj(i, k):
        return lax.dot_general(hs[i, :, chunk_cols(k)].astype(BF16), cblk_ref[k], NT_DIMS,
                               preferred_element_type=F32)

    xs, uts = [], []
    for i in range(n_sub):
        x = x_ref[:, i * tl:(i + 1) * tl, :].reshape(rows, D)
        u = _rms(x, g_ref[1:2, :])
        u_hi = u.astype(BF16)
        u_lo = (u - u_hi.astype(F32)).astype(BF16)
        ut_hi = _dot(perm_ref[...], u_hi)
        ut_b = ut_hi.astype(BF16)
        for k in range(S5_CHUNKS):
            hs[i, :, chunk_cols(k)] = _dot(ut_b[:, k * 256:(k + 1) * 256], bblk_ref[k])
        xs.append(x)
        uts.append(ut_hi + _dot(perm_ref[...], u_lo))

    ys = [[None] * S5_CHUNKS for _ in range(n_sub)]
    for i in range(n_sub):
        for k in range(S5_CHUNKS):
            _s5_scan_chunk_static(hs.at[i], st, are_ref, aim_ref, k, tl)
            if i > 0:
                ys[i - 1][k] = c_proj(i - 1, k)
    for k in range(S5_CHUNKS):
        ys[n_sub - 1][k] = c_proj(n_sub - 1, k)

    for i in range(n_sub):
        y = jnp.concatenate(ys[i], axis=-1) + d_ref[...] * uts[i]
        ge = jax.nn.gelu(y, approximate=True).astype(BF16)
        ge = _dot(permt_ref[...], ge).astype(BF16)
        out = _glu_out(_dot(ge, wglu_ref[...]))
        o_ref[:, i * tl:(i + 1) * tl, :] = (xs[i] + out).reshape(nb, tl, D)

    @pl.when(t == pl.num_programs(0) - 1)
    def _():
        _s5_state_out(st, hre_ref, him_ref)


def _s5_prompt(x3d, s5_p, h0re, h0im, *, tl):
    nb, l, _ = x3d.shape
    assert nb % 8 == 0 and nb & (nb - 1) == 0 and tl & (tl - 1) == 0
    rows = nb * tl
    n_sub = S5_SUB
    step = n_sub * tl
    kern = functools.partial(_s5_prompt_kernel, nb=nb, tl=tl, n_sub=n_sub)
    st_shape = jax.ShapeDtypeStruct((nb, SSM_LANES), F32)
    return pl.pallas_call(
        kern,
        grid=(l // step,),
        in_specs=[pl.BlockSpec((nb, step, D), lambda t: (0, t, 0))] + _s5_param_specs() + [
                  _const_spec((nb, SSM_LANES)),
                  _const_spec((nb, SSM_LANES))],
        out_specs=[pl.BlockSpec((nb, step, D), lambda t: (0, t, 0)),
                   pl.BlockSpec((nb, SSM_LANES), lambda t: (0, 0)),
                   pl.BlockSpec((nb, SSM_LANES), lambda t: (0, 0))],
        out_shape=[jax.ShapeDtypeStruct(x3d.shape, F32), st_shape, st_shape],
        scratch_shapes=[pltpu.VMEM((n_sub, rows, 2 * SSM_LANES), F32), pltpu.VMEM((nb, 2 * SSM_LANES), F32),
                        pltpu.VMEM((rows, rows), BF16), pltpu.VMEM((rows, rows), BF16)],
        compiler_params=_params(("arbitrary",), 56),
        name="s5_prompt",
    )(x3d, *s5_p, h0re, h0im)


def _s5_sample_kernel(x_ref, g_ref, bblk_ref, are_ref, aim_ref, cblk_ref, d_ref, wglu_ref,
                      h0re_ref, h0im_ref, o_ref, hre_ref, him_ref, hs, st, *, nb, seq_len):
    _s5_state_in(st, h0re_ref, h0im_ref)
    xt = jnp.concatenate([x_ref[:, t * D:(t + 1) * D] for t in range(seq_len)], axis=0)
    ut = _rms(xt, g_ref[1:2, :])
    ge = _s5_core(ut, ut.astype(BF16), bblk_ref, are_ref, aim_ref, cblk_ref, d_ref, hs, st, nb, seq_len)
    out = _glu_out(_dot(ge, wglu_ref[...]))
    for t in range(seq_len):
        o_ref[:, t * D:(t + 1) * D] = x_ref[:, t * D:(t + 1) * D] + out[t * nb:(t + 1) * nb, :]
    _s5_state_out(st, hre_ref, him_ref)


def _s5_sample(xv, s5_p, h0re, h0im, *, nb, seq_len):
    nseq = xv.shape[0]
    assert nseq % nb == 0 and nb % 8 == 0
    rows = nb * seq_len
    kern = functools.partial(_s5_sample_kernel, nb=nb, seq_len=seq_len)
    st_shape = jax.ShapeDtypeStruct((nseq, SSM_LANES), F32)
    st_spec = pl.BlockSpec((nb, SSM_LANES), lambda i: (i, 0))
    return pl.pallas_call(
        kern,
        grid=(nseq // nb,),
        in_specs=[pl.BlockSpec((nb, seq_len * D), lambda i: (i, 0))] + _s5_param_specs() + [st_spec, st_spec],
        out_specs=[pl.BlockSpec((nb, seq_len * D), lambda i: (i, 0)), st_spec, st_spec],
        out_shape=[jax.ShapeDtypeStruct(xv.shape, F32), st_shape, st_shape],
        scratch_shapes=[pltpu.VMEM((rows, 2 * SSM_LANES), F32), pltpu.VMEM((nb, 2 * SSM_LANES), F32)],
        compiler_params=_params(("arbitrary",), 48),
        name="s5_sample",
    )(xv, *s5_p, h0re, h0im)


def _s5_discretize(a_re, a_im, log_dt, b_re, b_im, c_re, c_im):
    dt = jnp.exp(log_dt)[:, None]
    mag = jnp.exp(dt * a_re)
    ab_re = mag * jnp.cos(dt * a_im)
    ab_im = mag * jnp.sin(dt * a_im)
    den = a_re * a_re + a_im * a_im
    num_re = ab_re - 1.0
    coef_re = (num_re * a_re + ab_im * a_im) / den
    coef_im = (ab_im * a_re - num_re * a_im) / den
    bb_re = coef_re[..., None] * b_re - coef_im[..., None] * b_im
    bb_im = coef_re[..., None] * b_im + coef_im[..., None] * b_re
    gpc = SSM_GROUPS // S5_CHUNKS
    cols = 2 * S5_CW
    col_group = (lax.broadcasted_iota(jnp.int32, (gpc, 1, cols), 2) // SSM_STATE) % gpc
    own = col_group == lax.broadcasted_iota(jnp.int32, (gpc, 1, cols), 0)

    def block_diag(compact):
        full = jnp.where(own[None], compact[:, None], 0.0).astype(BF16)
        return full.reshape(S5_CHUNKS, gpc * SSM_GROUP, cols)

    bb = jnp.stack([bb_re, bb_im]).reshape(2, S5_CHUNKS, gpc, SSM_STATE, SSM_GROUP)
    bblk = block_diag(jnp.transpose(bb, (1, 4, 0, 2, 3)).reshape(S5_CHUNKS, SSM_GROUP, cols))
    cc = jnp.stack([c_re, -c_im]).reshape(2, S5_CHUNKS, gpc, SSM_GROUP, SSM_STATE)
    cblk = block_diag(jnp.transpose(cc, (1, 3, 0, 2, 4)).reshape(S5_CHUNKS, SSM_GROUP, cols))
    are = jnp.broadcast_to(ab_re.reshape(1, SSM_LANES), (8, SSM_LANES))
    aim = jnp.broadcast_to(ab_im.reshape(1, SSM_LANES), (8, SSM_LANES))
    return bblk, cblk, are, aim


S5_TL = 32
S5_SUB = 2
CONV_TL = 256
CONV_NS = 1
ATTN_TQ = 1024
ATTN_SUB = 256
MLP_TM = 512


def kernel(x_prompt, x_sample, mem_prompt, cache_conv, state_s5_re, state_s5_im, cache_mem_k, cache_mem_v,
           norm_mix, norm_xattn, norm_ffn, norm_final,
           conv_w_in, conv_b_in, conv_dw, conv_dw_b, conv_ln_g, conv_ln_b, conv_w_out,
           s5_a_re, s5_a_im, s5_log_dt, s5_b_re, s5_b_im, s5_c_re, s5_c_im, s5_d, s5_w_glu,
           xattn_w_q, xattn_w_k, xattn_w_v, xattn_w_o, mlp_w_up, mlp_w_down):
    bp, lp, _ = x_prompt.shape
    bs, ls, _ = x_sample.shape

    conv_p = (norm_mix, conv_w_in.astype(BF16), conv_b_in, conv_dw, conv_dw_b, conv_ln_g, conv_ln_b,
              conv_w_out.astype(BF16))
    bblk, cblk, are, aim = _s5_discretize(s5_a_re[0], s5_a_im[0], s5_log_dt[0], s5_b_re[0], s5_b_im[0],
                                          s5_c_re[0], s5_c_im[0])
    s5_p = (norm_mix, bblk, are, aim, cblk, s5_d, s5_w_glu.astype(BF16))
    gfin = norm_final.reshape(1, D)

    kp, vp = _mem_kv(mem_prompt.reshape(bp * N_MEM, D), xattn_w_k, xattn_w_v)
    kp4 = kp.reshape(DEPTH, bp, KV_ROWS, LANES)
    vp4 = vp.reshape(DEPTH, bp, KV_ROWS, LANES)
    mem_k_prompt = _kv_rows_unview(kp4)
    mem_v_prompt = _kv_rows_unview(vp4)

    def attn_sample(x2d, i):
        return _attn_sample(x2d, norm_xattn, xattn_w_q, ck, cv, xattn_w_o, i, sb=4, seq_len=ls)

    def mlp_sample(x2d, i, final):
        return _mlp_streaming(x2d, norm_ffn, mlp_w_up, mlp_w_down, gfin, i, tm=MLP_TM, hid_steps=4,
                              final_norm=final)

    xp_conv, hist = _conv_prompt(x_prompt, conv_p, tl=CONV_TL)
    (xp_conv, hist), x_sample = lax.optimization_barrier(((xp_conv, hist), x_sample))

    ck = _kv_rows_view(cache_mem_k)
    cv = _kv_rows_view(cache_mem_v)
    hist3 = jnp.transpose(cache_conv[0], (1, 0, 2))
    xv, nhist = _conv_sample(x_sample.reshape(bs, ls * D), hist3, conv_p, nb=32, seq_len=ls)
    conv_sample = jnp.transpose(nhist, (1, 0, 2))[None]
    x, wq0, wo0 = attn_sample(xv.reshape(bs * ls, D), 0)
    x, wup0, wdn0 = mlp_sample(x, 0, False)
    xv, hre, him = _s5_sample(x.reshape(bs, ls * D), s5_p,
                              state_s5_re[0].reshape(bs, SSM_LANES), state_s5_im[0].reshape(bs, SSM_LANES),
                              nb=64, seq_len=ls)
    s5_re_sample = hre.reshape(1, bs, SSM_GROUPS, SSM_STATE)
    s5_im_sample = him.reshape(1, bs, SSM_GROUPS, SSM_STATE)
    x, wq1, wo1 = attn_sample(xv.reshape(bs * ls, D), 1)
    x, wup1, wdn1 = mlp_sample(x, 1, True)
    y_sample = x.reshape(bs, ls, D)

    def mlp_prompt(x3d, wup_b, wdn_b, i, final):
        y = _mlp_resident(x3d.reshape(bp * lp, D), norm_ffn, wup_b, wdn_b, gfin, i, tm=MLP_TM, final_norm=final)
        return y.reshape(bp, lp, D)

    conv_prompt = hist[None, :, HIST_PAD - CONV_BUF:, :]
    x = _attn_prompt(xp_conv, norm_xattn, wq0, kp4, vp4, wo0, 0, tq=ATTN_TQ)
    x = mlp_prompt(x, wup0, wdn0, 0, False)
    zeros_state = jnp.zeros((bp, SSM_LANES), F32)
    x, hre, him = _s5_prompt(x, s5_p, zeros_state, zeros_state, tl=S5_TL)
    s5_re_prompt = hre.reshape(1, bp, SSM_GROUPS, SSM_STATE)
    s5_im_prompt = him.reshape(1, bp, SSM_GROUPS, SSM_STATE)
    x = _attn_prompt(x, norm_xattn, wq1, kp4, vp4, wo1, 1, tq=ATTN_TQ)
    y_prompt = mlp_prompt(x, wup1, wdn1, 1, True)

    return (y_prompt, y_sample, conv_prompt, conv_sample, s5_re_prompt, s5_im_prompt,
            s5_re_sample, s5_im_sample, mem_k_prompt, mem_v_prompt)
```

```python
import functools

import jax
import jax.numpy as jnp
from jax import lax
from jax.experimental import pallas as pl
from jax.experimental.pallas import tpu as pltpu

F32 = jnp.float32
BF16 = jnp.bfloat16

LANES = 128
D = 1024
DEPTH = 2
D_FF = 4 * D
N_MEM = 256
HEADS = 4
HD = D // HEADS
HD_TILES = HD // LANES
KV_ROWS = N_MEM * HEADS * HD_TILES
CONV_W = 31
CONV_BUF = CONV_W - 1
HIST_PAD = 32
SSM_GROUP = 16
SSM_GROUPS = D // SSM_GROUP
SSM_STATE = 64
SSM_LANES = SSM_GROUPS * SSM_STATE
S5_CHUNKS = 4
S5_CW = SSM_LANES // S5_CHUNKS
EPS = 1e-6
ATTN_SCALE = HD ** -0.5
NT_DIMS = (((1,), (1,)), ((), ()))
MIB = 2 ** 20


def _dot(a, b):
    return jnp.dot(a, b, preferred_element_type=F32)


def _rms(x, g):
    ms = jnp.mean(x * x, axis=-1, keepdims=True)
    return x * lax.rsqrt(ms + EPS) * g


def _softmax(s):
    m = jnp.max(s, axis=-1, keepdims=True)
    e = jnp.exp(s - m)
    return e / jnp.sum(e, axis=-1, keepdims=True)


def _const_spec(shape):
    nd = len(shape)
    return pl.BlockSpec(shape, lambda *_: (0,) * nd, pipeline_mode=pl.Buffered(1))


def _layer_spec(shape, layer):
    nd = len(shape)
    return pl.BlockSpec((None,) + shape, lambda *_: (layer,) + (0,) * nd, pipeline_mode=pl.Buffered(1))


def _params(sem, vmem_mib):
    return pltpu.CompilerParams(dimension_semantics=sem, vmem_limit_bytes=vmem_mib * MIB)


def _kv_rows_view(kv5d):
    lead = kv5d.shape[:-3]
    v = kv5d.reshape(lead + (N_MEM, HEADS, HD_TILES, LANES))
    v = jnp.swapaxes(v, -3, -2)
    return v.reshape(lead + (KV_ROWS, LANES))


def _kv_rows_unview(rows4d):
    lead = rows4d.shape[:-2]
    v = rows4d.reshape(lead + (N_MEM, HD_TILES, HEADS, LANES))
    v = jnp.swapaxes(v, -3, -2)
    return v.reshape(lead + (N_MEM, HEADS, HD))


def _head_rows(half, h, n):
    return pl.ds(half * HEADS + h, n, stride=HEADS * HD_TILES)


def _load_head(ref, lead, h):
    parts = [ref[lead + (_head_rows(e, h, N_MEM), slice(None))] for e in range(HD_TILES)]
    return jnp.concatenate(parts, axis=1).astype(BF16)


def _kv_kernel(mem_ref, wk_ref, wv_ref, k_ref, v_ref, wk_sc, wv_sc, *, tm):
    @pl.when(pl.program_id(1) == 0)
    def _():
        wk_sc[...] = wk_ref[...].astype(BF16)
        wv_sc[...] = wv_ref[...].astype(BF16)

    m = mem_ref[...].astype(BF16)
    for w_sc, o_ref in ((wk_sc, k_ref), (wv_sc, v_ref)):
        y = _dot(m, w_sc[...])
        for h in range(HEADS):
            for e in range(HD_TILES):
                c0 = h * HD + e * LANES
                o_ref[0, _head_rows(e, h, tm), :] = y[:, c0:c0 + LANES]


def _mem_kv(mem2d, wk, wv):
    rows = mem2d.shape[0]
    tm = 512
    rpm = HEADS * HD_TILES
    out = jax.ShapeDtypeStruct((DEPTH, rows * rpm, LANES), F32)
    return pl.pallas_call(
        functools.partial(_kv_kernel, tm=tm),
        grid=(DEPTH, rows // tm),
        in_specs=[pl.BlockSpec((tm, D), lambda l, r: (r, 0)),
                  pl.BlockSpec((None, D, D), lambda l, r: (l, 0, 0)),
                  pl.BlockSpec((None, D, D), lambda l, r: (l, 0, 0))],
        out_specs=[pl.BlockSpec((1, tm * rpm, LANES), lambda l, r: (l, r, 0)),
                   pl.BlockSpec((1, tm * rpm, LANES), lambda l, r: (l, r, 0))],
        out_shape=[out, out],
        scratch_shapes=[pltpu.VMEM((D, D), BF16), pltpu.VMEM((D, D), BF16)],
        compiler_params=_params(("arbitrary", "arbitrary"), 40),
        name="mem_kv",
    )(mem2d, wk, wv)


MLP_CH = 1024


def _mlp_kernel(x_ref, g_ref, wup_ref, wdn_ref, gfin_ref, *rest, layer, n_inner, final_norm, emit_bf16):
    if emit_bf16:
        o_ref, wup_b, wdn_b, xn_sc, acc_sc = rest
        wup_b[...] = wup_ref[...].astype(BF16)
        wdn_b[...] = wdn_ref[...].astype(BF16)
    else:
        o_ref, xn_sc, acc_sc = rest
        wup_b, wdn_b = wup_ref, wdn_ref
    c = pl.program_id(1)

    @pl.when(c == 0)
    def _():
        x = x_ref[...]
        xn_sc[...] = _rms(x, g_ref[layer:layer + 1, :]).astype(BF16)
        acc_sc[...] = x

    xn = xn_sc[...]
    acc = acc_sc[...]
    for j in range(n_inner):
        h = _dot(xn, wup_b[:, j * MLP_CH:(j + 1) * MLP_CH])
        h = jnp.maximum(h, 0.0)
        acc = acc + _dot((h * h).astype(BF16), wdn_b[j * MLP_CH:(j + 1) * MLP_CH, :])
    acc_sc[...] = acc

    @pl.when(c == pl.num_programs(1) - 1)
    def _():
        if final_norm:
            o_ref[...] = _rms(acc, gfin_ref[...])
        else:
            o_ref[...] = acc


def _mlp_call(x2d, g, wup, wdn, gfin, layer, w_specs, extra_out_specs, extra_out_shapes, *, tm, hid_steps,
              final_norm):
    rows = x2d.shape[0]
    kern = functools.partial(_mlp_kernel, layer=layer, n_inner=D_FF // hid_steps // MLP_CH, final_norm=final_norm,
                             emit_bf16=bool(extra_out_specs))
    return pl.pallas_call(
        kern,
        grid=(rows // tm, hid_steps),
        in_specs=[pl.BlockSpec((tm, D), lambda r, c: (r, 0)), _const_spec((DEPTH, D))] + w_specs + [
                  _const_spec((1, D))],
        out_specs=[pl.BlockSpec((tm, D), lambda r, c: (r, 0))] + extra_out_specs,
        out_shape=[jax.ShapeDtypeStruct((rows, D), F32)] + extra_out_shapes,
        scratch_shapes=[pltpu.VMEM((tm, D), BF16), pltpu.VMEM((tm, D), F32)],
        compiler_params=_params(("arbitrary", "arbitrary"), 48),
        name="mlp",
    )(x2d, g, wup, wdn, gfin)


def _mlp_resident(x2d, g, wup_b, wdn_b, gfin, layer, *, tm, final_norm):
    w_specs = [_const_spec((D, D_FF)), _const_spec((D_FF, D))]
    return _mlp_call(x2d, g, wup_b, wdn_b, gfin, layer, w_specs, [], [], tm=tm, hid_steps=1,
                     final_norm=final_norm)[0]


def _mlp_streaming(x2d, g, wup, wdn, gfin, layer, *, tm, hid_steps, final_norm):
    hid_blk = D_FF // hid_steps
    w_specs = [pl.BlockSpec((None, D, hid_blk), lambda r, c: (layer, 0, c)),
               pl.BlockSpec((None, hid_blk, D), lambda r, c: (layer, c, 0))]
    b_specs = [pl.BlockSpec((D, hid_blk), lambda r, c: (0, c)), pl.BlockSpec((hid_blk, D), lambda r, c: (c, 0))]
    b_shapes = [jax.ShapeDtypeStruct((D, D_FF), BF16), jax.ShapeDtypeStruct((D_FF, D), BF16)]
    return _mlp_call(x2d, g, wup, wdn, gfin, layer, w_specs, b_specs, b_shapes, tm=tm, hid_steps=hid_steps,
                     final_norm=final_norm)


def _attn_prompt_kernel(x_ref, g_ref, wq_ref, k_ref, v_ref, wo_ref, o_ref, *, layer, tq, sub):
    starts = range(0, tq, sub)
    xs = [x_ref[0, r0:r0 + sub, :] for r0 in starts]
    qs = [(_dot(_rms(x, g_ref[layer:layer + 1, :]).astype(BF16), wq_ref[...]) * ATTN_SCALE).astype(BF16) for x in xs]
    outs = [[] for _ in starts]
    for h in range(HEADS):
        cols = slice(h * HD, (h + 1) * HD)
        kh = _load_head(k_ref, (), h)
        vh = _load_head(v_ref, (), h)
        for q, out in zip(qs, outs):
            s = lax.dot_general(q[:, cols], kh, NT_DIMS, preferred_element_type=F32)
            out.append(_dot(_softmax(s).astype(BF16), vh).astype(BF16))
    for r0, x, out in zip(starts, xs, outs):
        o_ref[0, r0:r0 + sub, :] = x + _dot(jnp.concatenate(out, axis=-1), wo_ref[...])


def _attn_prompt(x3d, g, wq, k4d, v4d, wo, layer, *, tq):
    b, l, _ = x3d.shape
    kv_spec = pl.BlockSpec((None, None, KV_ROWS, LANES), lambda i, t: (layer, i, 0, 0))
    return pl.pallas_call(
        functools.partial(_attn_prompt_kernel, layer=layer, tq=tq, sub=ATTN_SUB),
        grid=(b, l // tq),
        in_specs=[pl.BlockSpec((1, tq, D), lambda i, t: (i, t, 0)),
                  _const_spec((DEPTH, D)),
                  _const_spec((D, D)),
                  kv_spec, kv_spec,
                  _const_spec((D, D))],
        out_specs=pl.BlockSpec((1, tq, D), lambda i, t: (i, t, 0)),
        out_shape=jax.ShapeDtypeStruct(x3d.shape, F32),
        compiler_params=_params(("arbitrary", "arbitrary"), 40),
        name="attn_prompt",
    )(x3d, g, wq, k4d, v4d, wo)


def _attn_sample_kernel(x_ref, g_ref, wq_ref, k_ref, v_ref, wo_ref, o_ref, wq_b, wo_b, q_sc, o_sc, *,
                        layer, sb, seq_len):
    i = pl.program_id(0)

    @pl.when(i == 0)
    def _():
        wq_b[...] = wq_ref[...].astype(BF16)
        wo_b[...] = wo_ref[...].astype(BF16)
        xn = _rms(x_ref[...], g_ref[layer:layer + 1, :]).astype(BF16)
        q_sc[...] = _dot(xn, wq_b[...]) * ATTN_SCALE

    pair_rows = 2 * seq_len
    first_seq_row = lax.broadcasted_iota(jnp.int32, (pair_rows, HD), 0) < seq_len

    for j in range(sb // 2):
        r0 = pl.multiple_of((i * sb + 2 * j) * seq_len, pair_rows)
        q8 = q_sc[pl.ds(r0, pair_rows), :].astype(BF16)
        scores = [lax.dot_general(q8[:, h * HD:(h + 1) * HD], _load_head(k_ref, (2 * j + s,), h), NT_DIMS,
                                  preferred_element_type=F32)
                  for s in range(2) for h in range(HEADS)]
        p = _softmax(jnp.concatenate(scores, axis=0))
        res = []
        for h in range(HEADS):
            oh = []
            for s in range(2):
                blk = (s * HEADS + h) * pair_rows
                oh.append(_dot(p[blk:blk + pair_rows, :].astype(BF16), _load_head(v_ref, (2 * j + s,), h)))
            res.append(jnp.where(first_seq_row, oh[0], oh[1]))
        o_sc[pl.ds(r0, pair_rows), :] = jnp.concatenate(res, axis=-1)

    @pl.when(i == pl.num_programs(0) - 1)
    def _():
        o_ref[...] = x_ref[...] + _dot(o_sc[...].astype(BF16), wo_b[...])


def _attn_sample(x2d, g, wq, k4d, v4d, wo, layer, *, sb, seq_len):
    rows = x2d.shape[0]
    nseq = rows // seq_len
    assert 2 * seq_len == 8 and nseq % sb == 0 and sb % 2 == 0
    kern = functools.partial(_attn_sample_kernel, layer=layer, sb=sb, seq_len=seq_len)
    kv_spec = pl.BlockSpec((None, sb, KV_ROWS, LANES), lambda i: (layer, i, 0, 0))
    return pl.pallas_call(
        kern,
        grid=(nseq // sb,),
        in_specs=[_const_spec((rows, D)),
                  _const_spec((DEPTH, D)),
                  _layer_spec((D, D), layer),
                  kv_spec, kv_spec,
                  _layer_spec((D, D), layer)],
        out_specs=[pl.BlockSpec((rows, D), lambda i: (0, 0)),
                   pl.BlockSpec((D, D), lambda i: (0, 0)),
                   pl.BlockSpec((D, D), lambda i: (0, 0))],
        out_shape=[jax.ShapeDtypeStruct((rows, D), F32),
                   jax.ShapeDtypeStruct((D, D), BF16),
                   jax.ShapeDtypeStruct((D, D), BF16)],
        scratch_shapes=[pltpu.VMEM((rows, D), F32), pltpu.VMEM((rows, D), F32)],
        compiler_params=_params(("arbitrary",), 56),
        name="attn_sample",
    )(x2d, g, wq, k4d, v4d, wo)


def _glu_in(x, g, win_ref, bin_ref):
    xn = _rms(x, g).astype(BF16)
    h = _dot(xn, win_ref[...]) + bin_ref[...]
    return h[:, :D] * jax.nn.sigmoid(h[:, D:])


def _ln_silu_out(c, lng_ref, lnb_ref, wout_ref):
    mu = jnp.mean(c, axis=-1, keepdims=True)
    cc = c - mu
    var = jnp.mean(cc * cc, axis=-1, keepdims=True)
    n = cc * lax.rsqrt(var + EPS) * lng_ref[...] + lnb_ref[...]
    s = (n * jax.nn.sigmoid(n)).astype(BF16)
    return _dot(s, wout_ref[...])


def _conv_prompt_kernel(x_ref, g_ref, win_ref, bin_ref, dw_ref, dwb_ref, lng_ref, lnb_ref, wout_ref,
                        o_ref, hist_ref, gbuf, cbuf, *, tl, ns):
    n_tiles = D // LANES
    n_groups = tl // 8
    base = HIST_PAD - CONV_BUF

    def rows2(start, n):
        return pl.ds(2 * start, n, stride=2)

    @pl.when(pl.program_id(1) == 0)
    def _():
        for s in range(ns):
            for j in range(n_tiles):
                gbuf[s, j, rows2(0, HIST_PAD), :] = jnp.zeros((HIST_PAD, LANES), F32)

    xs = [x_ref[s] for s in range(ns)]
    xns = [_rms(x, g_ref[0:1, :]).astype(BF16) for x in xs]

    def glu_block(s, c):
        va = slice(c * 256, (c + 1) * 256)
        ga = slice(D + c * 256, D + (c + 1) * 256)
        val = _dot(xns[s], win_ref[:, va]) + bin_ref[:, va]
        gate = _dot(xns[s], win_ref[:, ga]) + bin_ref[:, ga]
        gl = val * jax.nn.sigmoid(gate)
        for jj in range(256 // LANES):
            gbuf[s, 2 * c + jj, rows2(HIST_PAD, tl), :] = gl[:, jj * LANES:(jj + 1) * LANES]

    def conv_tile(s, j):
        lanes = slice(j * LANES, (j + 1) * LANES)
        w = [jnp.broadcast_to(dw_ref[k:k + 1, lanes], (8, LANES)) for k in range(CONV_W)]
        bias = jnp.broadcast_to(dwb_ref[:, lanes], (8, LANES))
        accs = [bias] * 4
        for i in range(n_groups + 3):
            lags = range(max(0, i - n_groups + 1), min(i, 3) + 1)
            for r in range(8):
                taps = [(a, 8 * a + r) for a in lags if 8 * a + r < CONV_W]
                if taps:
                    wnd = gbuf[s, j, rows2(base + 8 * i + r, 8), :]
                    for a, k in taps:
                        accs[a] = accs[a] + w[k] * wnd
            if i >= 3:
                cbuf[s, (i - 3) * 8:(i - 2) * 8, lanes] = accs[3]
            accs = [bias, accs[0], accs[1], accs[2]]

    def finish(s):
        o_ref[s] = xs[s] + _ln_silu_out(cbuf[s], lng_ref, lnb_ref, wout_ref)
        for j in range(n_tiles):
            tail = gbuf[s, j, rows2(tl, HIST_PAD), :]
            hist_ref[s, :, j * LANES:(j + 1) * LANES] = tail
            gbuf[s, j, rows2(0, HIST_PAD), :] = tail

    for c in range(D // 256):
        glu_block(0, c)
    for s in range(ns):
        for j in range(n_tiles):
            conv_tile(s, j)
            if s + 1 < ns and j % 2 == 1:
                glu_block(s + 1, j // 2)
        finish(s)


def _conv_param_specs():
    return [_const_spec((DEPTH, D)),
            _layer_spec((D, 2 * D), 0),
            _const_spec((1, 2 * D)),
            _layer_spec((CONV_W, D), 0),
            _const_spec((1, D)),
            _const_spec((1, D)),
            _const_spec((1, D)),
            _layer_spec((D, D), 0)]


def _conv_prompt(x3d, conv_p, *, tl):
    b, l, _ = x3d.shape
    ns = CONV_NS
    assert b % ns == 0
    kern = functools.partial(_conv_prompt_kernel, tl=tl, ns=ns)
    return pl.pallas_call(
        kern,
        grid=(b // ns, l // tl),
        in_specs=[pl.BlockSpec((ns, tl, D), lambda i, t: (i, t, 0))] + _conv_param_specs(),
        out_specs=[pl.BlockSpec((ns, tl, D), lambda i, t: (i, t, 0)),
                   pl.BlockSpec((ns, HIST_PAD, D), lambda i, t: (i, 0, 0))],
        out_shape=[jax.ShapeDtypeStruct(x3d.shape, F32),
                   jax.ShapeDtypeStruct((b, HIST_PAD, D), F32)],
        scratch_shapes=[pltpu.VMEM((ns, D // LANES, 2 * (HIST_PAD + tl), LANES), F32),
                        pltpu.VMEM((ns, tl, D), F32)],
        compiler_params=_params(("arbitrary", "arbitrary"), 40),
        name="conv_prompt",
    )(x3d, *conv_p)


def _conv_sample_kernel(x_ref, hist_ref, g_ref, win_ref, bin_ref, dw_ref, dwb_ref, lng_ref, lnb_ref, wout_ref,
                        o_ref, nhist_ref, *, nb, seq_len):
    xt = jnp.concatenate([x_ref[:, t * D:(t + 1) * D] for t in range(seq_len)], axis=0)
    gl = _glu_in(xt, g_ref[0:1, :], win_ref, bin_ref)
    g_t = [gl[t * nb:(t + 1) * nb, :] for t in range(seq_len)]

    def padded(r, lanes):
        if r < CONV_BUF:
            return hist_ref[r, :, lanes]
        return g_t[r - CONV_BUF][:, lanes]

    cs = []
    for t in range(seq_len):
        tiles = []
        for j in range(D // LANES):
            lanes = slice(j * LANES, (j + 1) * LANES)
            acc = None
            for k in range(CONV_W):
                term = dw_ref[k:k + 1, lanes] * padded(t + k, lanes)
                acc = term if acc is None else acc + term
            tiles.append(acc)
        cs.append(jnp.concatenate(tiles, axis=-1))
    c = jnp.concatenate(cs, axis=0) + dwb_ref[...]
    res = _ln_silu_out(c, lng_ref, lnb_ref, wout_ref)
    for t in range(seq_len):
        o_ref[:, t * D:(t + 1) * D] = x_ref[:, t * D:(t + 1) * D] + res[t * nb:(t + 1) * nb, :]
    keep = CONV_BUF - seq_len
    nhist_ref[0:keep] = hist_ref[seq_len:CONV_BUF]
    for t in range(seq_len):
        nhist_ref[keep + t] = g_t[t]


def _conv_sample(xv, hist3, conv_p, *, nb, seq_len):
    nseq = xv.shape[0]
    assert seq_len <= CONV_BUF and nseq % nb == 0 and nb % 8 == 0
    kern = functools.partial(_conv_sample_kernel, nb=nb, seq_len=seq_len)
    hist_spec = pl.BlockSpec((CONV_BUF, nb, D), lambda i: (0, i, 0))
    return pl.pallas_call(
        kern,
        grid=(nseq // nb,),
        in_specs=[pl.BlockSpec((nb, seq_len * D), lambda i: (i, 0)), hist_spec] + _conv_param_specs(),
        out_specs=[pl.BlockSpec((nb, seq_len * D), lambda i: (i, 0)), hist_spec],
        out_shape=[jax.ShapeDtypeStruct(xv.shape, F32),
                   jax.ShapeDtypeStruct(hist3.shape, F32)],
        compiler_params=_params(("arbitrary",), 40),
        name="conv_sample",
    )(xv, hist3, *conv_p)


def _s5_scan_chunk_static(hs, st, are_ref, aim_ref, k, tl):
    ar = are_ref[:, k * S5_CW:(k + 1) * S5_CW]
    ai = aim_ref[:, k * S5_CW:(k + 1) * S5_CW]
    cre = slice(2 * k * S5_CW, (2 * k + 1) * S5_CW)
    cim = slice((2 * k + 1) * S5_CW, (2 * k + 2) * S5_CW)
    hr, hi = st[:, cre], st[:, cim]
    for t in range(tl):
        rows = slice(t * 8, (t + 1) * 8)
        hr, hi = ar * hr - ai * hi + hs[rows, cre], ar * hi + ai * hr + hs[rows, cim]
        hs[rows, cre] = hr
        hs[rows, cim] = hi
    st[:, cre] = hr
    st[:, cim] = hi


def _s5_scan(hs, st, are_ref, aim_ref, nb, tl):
    for k in range(S5_CHUNKS):
        ar = are_ref[:, k * S5_CW:(k + 1) * S5_CW]
        ai = aim_ref[:, k * S5_CW:(k + 1) * S5_CW]
        cre = slice(2 * k * S5_CW, (2 * k + 1) * S5_CW)
        cim = slice((2 * k + 1) * S5_CW, (2 * k + 2) * S5_CW)

        def seq_group(bg, carry, ar=ar, ai=ai, cre=cre, cim=cim):
            b0 = pl.multiple_of(bg * 8, 8)

            def step(t, h):
                hr, hi = h
                row = pl.multiple_of(t * nb + b0, 8)
                nr = ar * hr - ai * hi + hs[pl.ds(row, 8), cre]
                ni = ar * hi + ai * hr + hs[pl.ds(row, 8), cim]
                hs[pl.ds(row, 8), cre] = nr
                hs[pl.ds(row, 8), cim] = ni
                return nr, ni

            hr, hi = lax.fori_loop(0, tl, step, (st[pl.ds(b0, 8), cre], st[pl.ds(b0, 8), cim]),
                                   unroll=min(tl, 4))
            st[pl.ds(b0, 8), cre] = hr
            st[pl.ds(b0, 8), cim] = hi
            return carry

        lax.fori_loop(0, nb // 8, seq_group, 0)


def _s5_core(ut, ut_b, bblk_ref, are_ref, aim_ref, cblk_ref, d_ref, hs, st, nb, tl):
    for k in range(S5_CHUNKS):
        hs[:, 2 * k * S5_CW:(2 * k + 2) * S5_CW] = _dot(ut_b[:, k * 256:(k + 1) * 256], bblk_ref[k])
    if nb == 8:
        for k in range(S5_CHUNKS):
            _s5_scan_chunk_static(hs, st, are_ref, aim_ref, k, tl)
    else:
        _s5_scan(hs, st, are_ref, aim_ref, nb, tl)
    ys = [lax.dot_general(hs[:, 2 * k * S5_CW:(2 * k + 2) * S5_CW].astype(BF16), cblk_ref[k], NT_DIMS,
                          preferred_element_type=F32) for k in range(S5_CHUNKS)]
    y = jnp.concatenate(ys, axis=-1) + d_ref[...] * ut
    return jax.nn.gelu(y, approximate=True).astype(BF16)


def _s5_state_in(st, h0re_ref, h0im_ref):
    for k in range(S5_CHUNKS):
        st[:, 2 * k * S5_CW:(2 * k + 1) * S5_CW] = h0re_ref[:, k * S5_CW:(k + 1) * S5_CW]
        st[:, (2 * k + 1) * S5_CW:(2 * k + 2) * S5_CW] = h0im_ref[:, k * S5_CW:(k + 1) * S5_CW]


def _s5_state_out(st, hre_ref, him_ref):
    for k in range(S5_CHUNKS):
        hre_ref[:, k * S5_CW:(k + 1) * S5_CW] = st[:, 2 * k * S5_CW:(2 * k + 1) * S5_CW]
        him_ref[:, k * S5_CW:(k + 1) * S5_CW] = st[:, (2 * k + 1) * S5_CW:(2 * k + 2) * S5_CW]


def _glu_out(z):
    return z[:, :D] * jax.nn.sigmoid(z[:, D:])


def _s5_param_specs():
    return [_const_spec((DEPTH, D)),
            _const_spec((S5_CHUNKS, 256, 2 * S5_CW)),
            _const_spec((8, SSM_LANES)),
            _const_spec((8, SSM_LANES)),
            _const_spec((S5_CHUNKS, 256, 2 * S5_CW)),
            _const_spec((1, D)),
            _layer_spec((D, 2 * D), 0)]


def _s5_prompt_kernel(x_ref, g_ref, bblk_ref, are_ref, aim_ref, cblk_ref, d_ref, wglu_ref,
                      h0re_ref, h0im_ref, o_ref, hre_ref, him_ref, hs, st, ubuf, gtmp, *, nb, tl, n_sub):
    t = pl.program_id(0)
    rows = nb * tl
    n_tiles = D // LANES
    pitch = tl + 8

    @pl.when(t == 0)
    def _():
        _s5_state_in(st, h0re_ref, h0im_ref)

    def chunk_cols(k):
        return slice(2 * k * S5_CW, (2 * k + 2) * S5_CW)

    def c_proj(i, k):
        return lax.dot_general(hs[i, :, chunk_cols(k)].astype(BF16), cblk_ref[k], NT_DIMS,
                               preferred_element_type=F32)

    xs, uts = [], []
    for i in range(n_sub):
        x = x_ref[:, i * tl:(i + 1) * tl, :].reshape(rows, D)
        u = _rms(x, g_ref[1:2, :])
        for j in range(n_tiles):
            for b in range(nb):
                ubuf[i, j, b * pitch:b * pitch + tl, :] = u[b * tl:(b + 1) * tl, j * LANES:(j + 1) * LANES]
        ut = jnp.concatenate(
            [jnp.concatenate([ubuf[i, j, pl.ds(step, nb, stride=pitch), :] for j in range(n_tiles)], axis=1)
             for step in range(tl)], axis=0)
        ut_b = ut.astype(BF16)
        for k in range(S5_CHUNKS):
            hs[i, :, chunk_cols(k)] = _dot(ut_b[:, k * 256:(k + 1) * 256], bblk_ref[k])
        xs.append(x)
        uts.append(ut)

    ys = [[None] * S5_CHUNKS for _ in range(n_sub)]
    for i in range(n_sub):
        for k in range(S5_CHUNKS):
            _s5_scan_chunk_static(hs.at[i], st, are_ref, aim_ref, k, tl)
            if i > 0:
                ys[i - 1][k] = c_proj(i - 1, k)
    for k in range(S5_CHUNKS):
        ys[n_sub - 1][k] = c_proj(n_sub - 1, k)

    for i in range(n_sub):
        y = jnp.concatenate(ys[i], axis=-1) + d_ref[...] * uts[i]
        ge = jax.nn.gelu(y, approximate=True)
        for j in range(n_tiles):
            gtmp[i, j] = ge[:, j * LANES:(j + 1) * LANES]
        ge = jnp.concatenate(
            [jnp.concatenate([gtmp[i, j, pl.ds(b, tl, stride=nb), :] for j in range(n_tiles)], axis=1)
             for b in range(nb)], axis=0).astype(BF16)
        out = _glu_out(_dot(ge, wglu_ref[...]))
        o_ref[:, i * tl:(i + 1) * tl, :] = (xs[i] + out).reshape(nb, tl, D)

    @pl.when(t == pl.num_programs(0) - 1)
    def _():
        _s5_state_out(st, hre_ref, him_ref)


def _s5_prompt(x3d, s5_p, h0re, h0im, *, tl):
    nb, l, _ = x3d.shape
    assert nb % 8 == 0 and nb & (nb - 1) == 0 and tl & (tl - 1) == 0
    rows = nb * tl
    n_sub = S5_SUB
    step = n_sub * tl
    kern = functools.partial(_s5_prompt_kernel, nb=nb, tl=tl, n_sub=n_sub)
    st_shape = jax.ShapeDtypeStruct((nb, SSM_LANES), F32)
    return pl.pallas_call(
        kern,
        grid=(l // step,),
        in_specs=[pl.BlockSpec((nb, step, D), lambda t: (0, t, 0))] + _s5_param_specs() + [
                  _const_spec((nb, SSM_LANES)),
                  _const_spec((nb, SSM_LANES))],
        out_specs=[pl.BlockSpec((nb, step, D), lambda t: (0, t, 0)),
                   pl.BlockSpec((nb, SSM_LANES), lambda t: (0, 0)),
                   pl.BlockSpec((nb, SSM_LANES), lambda t: (0, 0))],
        out_shape=[jax.ShapeDtypeStruct(x3d.shape, F32), st_shape, st_shape],
        scratch_shapes=[pltpu.VMEM((n_sub, rows, 2 * SSM_LANES), F32), pltpu.VMEM((nb, 2 * SSM_LANES), F32),
                        pltpu.VMEM((n_sub, D // LANES, nb * (tl + 8), LANES), F32),
                        pltpu.VMEM((n_sub, D // LANES, rows, LANES), F32)],
        compiler_params=_params(("arbitrary",), 56),
        name="s5_prompt",
    )(x3d, *s5_p, h0re, h0im)


def _s5_sample_kernel(x_ref, g_ref, bblk_ref, are_ref, aim_ref, cblk_ref, d_ref, wglu_ref,
                      h0re_ref, h0im_ref, o_ref, hre_ref, him_ref, hs, st, *, nb, seq_len):
    _s5_state_in(st, h0re_ref, h0im_ref)
    xt = jnp.concatenate([x_ref[:, t * D:(t + 1) * D] for t in range(seq_len)], axis=0)
    ut = _rms(xt, g_ref[1:2, :])
    ge = _s5_core(ut, ut.astype(BF16), bblk_ref, are_ref, aim_ref, cblk_ref, d_ref, hs, st, nb, seq_len)
    out = _glu_out(_dot(ge, wglu_ref[...]))
    for t in range(seq_len):
        o_ref[:, t * D:(t + 1) * D] = x_ref[:, t * D:(t + 1) * D] + out[t * nb:(t + 1) * nb, :]
    _s5_state_out(st, hre_ref, him_ref)


def _s5_sample(xv, s5_p, h0re, h0im, *, nb, seq_len):
    nseq = xv.shape[0]
    assert nseq % nb == 0 and nb % 8 == 0
    rows = nb * seq_len
    kern = functools.partial(_s5_sample_kernel, nb=nb, seq_len=seq_len)
    st_shape = jax.ShapeDtypeStruct((nseq, SSM_LANES), F32)
    st_spec = pl.BlockSpec((nb, SSM_LANES), lambda i: (i, 0))
    return pl.pallas_call(
        kern,
        grid=(nseq // nb,),
        in_specs=[pl.BlockSpec((nb, seq_len * D), lambda i: (i, 0))] + _s5_param_specs() + [st_spec, st_spec],
        out_specs=[pl.BlockSpec((nb, seq_len * D), lambda i: (i, 0)), st_spec, st_spec],
        out_shape=[jax.ShapeDtypeStruct(xv.shape, F32), st_shape, st_shape],
        scratch_shapes=[pltpu.VMEM((rows, 2 * SSM_LANES), F32), pltpu.VMEM((nb, 2 * SSM_LANES), F32)],
        compiler_params=_params(("arbitrary",), 48),
        name="s5_sample",
    )(xv, *s5_p, h0re, h0im)


def _s5_discretize(a_re, a_im, log_dt, b_re, b_im, c_re, c_im):
    dt = jnp.exp(log_dt)[:, None]
    mag = jnp.exp(dt * a_re)
    ab_re = mag * jnp.cos(dt * a_im)
    ab_im = mag * jnp.sin(dt * a_im)
    den = a_re * a_re + a_im * a_im
    num_re = ab_re - 1.0
    coef_re = (num_re * a_re + ab_im * a_im) / den
    coef_im = (ab_im * a_re - num_re * a_im) / den
    bb_re = coef_re[..., None] * b_re - coef_im[..., None] * b_im
    bb_im = coef_re[..., None] * b_im + coef_im[..., None] * b_re
    gpc = SSM_GROUPS // S5_CHUNKS
    cols = 2 * S5_CW
    col_group = (lax.broadcasted_iota(jnp.int32, (gpc, 1, cols), 2) // SSM_STATE) % gpc
    own = col_group == lax.broadcasted_iota(jnp.int32, (gpc, 1, cols), 0)

    def block_diag(compact):
        full = jnp.where(own[None], compact[:, None], 0.0).astype(BF16)
        return full.reshape(S5_CHUNKS, gpc * SSM_GROUP, cols)

    bb = jnp.stack([bb_re, bb_im]).reshape(2, S5_CHUNKS, gpc, SSM_STATE, SSM_GROUP)
    bblk = block_diag(jnp.transpose(bb, (1, 4, 0, 2, 3)).reshape(S5_CHUNKS, SSM_GROUP, cols))
    cc = jnp.stack([c_re, -c_im]).reshape(2, S5_CHUNKS, gpc, SSM_GROUP, SSM_STATE)
    cblk = block_diag(jnp.transpose(cc, (1, 3, 0, 2, 4)).reshape(S5_CHUNKS, SSM_GROUP, cols))
    are = jnp.broadcast_to(ab_re.reshape(1, SSM_LANES), (8, SSM_LANES))
    aim = jnp.broadcast_to(ab_im.reshape(1, SSM_LANES), (8, SSM_LANES))
    return bblk, cblk, are, aim


S5_TL = 32
S5_SUB = 2
CONV_TL = 256
CONV_NS = 1
ATTN_TQ = 1024
ATTN_SUB = 256
MLP_TM = 512


def kernel(x_prompt, x_sample, mem_prompt, cache_conv, state_s5_re, state_s5_im, cache_mem_k, cache_mem_v,
           norm_mix, norm_xattn, norm_ffn, norm_final,
           conv_w_in, conv_b_in, conv_dw, conv_dw_b, conv_ln_g, conv_ln_b, conv_w_out,
           s5_a_re, s5_a_im, s5_log_dt, s5_b_re, s5_b_im, s5_c_re, s5_c_im, s5_d, s5_w_glu,
           xattn_w_q, xattn_w_k, xattn_w_v, xattn_w_o, mlp_w_up, mlp_w_down):
    bp, lp, _ = x_prompt.shape
    bs, ls, _ = x_sample.shape

    conv_p = (norm_mix, conv_w_in.astype(BF16), conv_b_in, conv_dw, conv_dw_b, conv_ln_g, conv_ln_b,
              conv_w_out.astype(BF16))
    bblk, cblk, are, aim = _s5_discretize(s5_a_re[0], s5_a_im[0], s5_log_dt[0], s5_b_re[0], s5_b_im[0],
                                          s5_c_re[0], s5_c_im[0])
    s5_p = (norm_mix, bblk, are, aim, cblk, s5_d, s5_w_glu.astype(BF16))
    gfin = norm_final.reshape(1, D)

    kp, vp = _mem_kv(mem_prompt.reshape(bp * N_MEM, D), xattn_w_k, xattn_w_v)
    kp4 = kp.reshape(DEPTH, bp, KV_ROWS, LANES)
    vp4 = vp.reshape(DEPTH, bp, KV_ROWS, LANES)
    mem_k_prompt = _kv_rows_unview(kp4)
    mem_v_prompt = _kv_rows_unview(vp4)

    def attn_sample(x2d, i):
        return _attn_sample(x2d, norm_xattn, xattn_w_q, ck, cv, xattn_w_o, i, sb=4, seq_len=ls)

    def mlp_sample(x2d, i, final):
        return _mlp_streaming(x2d, norm_ffn, mlp_w_up, mlp_w_down, gfin, i, tm=MLP_TM, hid_steps=4,
                              final_norm=final)

    xp_conv, hist = _conv_prompt(x_prompt, conv_p, tl=CONV_TL)
    (xp_conv, hist), x_sample = lax.optimization_barrier(((xp_conv, hist), x_sample))

    ck = _kv_rows_view(cache_mem_k)
    cv = _kv_rows_view(cache_mem_v)
    hist3 = jnp.transpose(cache_conv[0], (1, 0, 2))
    xv, nhist = _conv_sample(x_sample.reshape(bs, ls * D), hist3, conv_p, nb=32, seq_len=ls)
    conv_sample = jnp.transpose(nhist, (1, 0, 2))[None]
    x, wq0, wo0 = attn_sample(xv.reshape(bs * ls, D), 0)
    x, wup0, wdn0 = mlp_sample(x, 0, False)
    xv, hre, him = _s5_sample(x.reshape(bs, ls * D), s5_p,
                              state_s5_re[0].reshape(bs, SSM_LANES), state_s5_im[0].reshape(bs, SSM_LANES),
                              nb=64, seq_len=ls)
    s5_re_sample = hre.reshape(1, bs, SSM_GROUPS, SSM_STATE)
    s5_im_sample = him.reshape(1, bs, SSM_GROUPS, SSM_STATE)
    x, wq1, wo1 = attn_sample(xv.reshape(bs * ls, D), 1)
    x, wup1, wdn1 = mlp_sample(x, 1, True)
    y_sample = x.reshape(bs, ls, D)

    def mlp_prompt(x3d, wup_b, wdn_b, i, final):
        y = _mlp_resident(x3d.reshape(bp * lp, D), norm_ffn, wup_b, wdn_b, gfin, i, tm=MLP_TM, final_norm=final)
        return y.reshape(bp, lp, D)

    conv_prompt = hist[None, :, HIST_PAD - CONV_BUF:, :]
    x = _attn_prompt(xp_conv, norm_xattn, wq0, kp4, vp4, wo0, 0, tq=ATTN_TQ)
    x = mlp_prompt(x, wup0, wdn0, 0, False)
    zeros_state = jnp.zeros((bp, SSM_LANES), F32)
    x, hre, him = _s5_prompt(x, s5_p, zeros_state, zeros_state, tl=S5_TL)
    s5_re_prompt = hre.reshape(1, bp, SSM_GROUPS, SSM_STATE)
    s5_im_prompt = him.reshape(1, bp, SSM_GROUPS, SSM_STATE)
    x = _attn_prompt(x, norm_xattn, wq1, kp4, vp4, wo1, 1, tq=ATTN_TQ)
    y_prompt = mlp_prompt(x, wup1, wdn1, 1, True)

    return (y_prompt, y_sample, conv_prompt, conv_sample, s5_re_prompt, s5_im_prompt,
            s5_re_sample, s5_im_sample, mem_k_prompt, mem_v_prompt)
```

```python
import functools

import jax
import jax.numpy as jnp
from jax import lax
from jax.experimental import pallas as pl
from jax.experimental.pallas import tpu as pltpu

F32 = jnp.float32
BF16 = jnp.bfloat16

LANES = 128
D = 1024
DEPTH = 2
D_FF = 4 * D
N_MEM = 256
HEADS = 4
HD = D // HEADS
HD_TILES = HD // LANES
KV_ROWS = N_MEM * HEADS * HD_TILES
CONV_W = 31
CONV_BUF = CONV_W - 1
HIST_PAD = 32
SSM_GROUP = 16
SSM_GROUPS = D // SSM_GROUP
SSM_STATE = 64
SSM_LANES = SSM_GROUPS * SSM_STATE
S5_CHUNKS = 4
S5_CW = SSM_LANES // S5_CHUNKS
EPS = 1e-6
ATTN_SCALE = HD ** -0.5
NT_DIMS = (((1,), (1,)), ((), ()))
MIB = 2 ** 20


def _dot(a, b):
    return jnp.dot(a, b, preferred_element_type=F32)


def _rms(x, g):
    ms = jnp.mean(x * x, axis=-1, keepdims=True)
    return x * lax.rsqrt(ms + EPS) * g


def _softmax(s):
    m = jnp.max(s, axis=-1, keepdims=True)
    e = jnp.exp(s - m)
    return e / jnp.sum(e, axis=-1, keepdims=True)


def _const_spec(shape):
    nd = len(shape)
    return pl.BlockSpec(shape, lambda *_: (0,) * nd, pipeline_mode=pl.Buffered(1))


def _layer_spec(shape, layer):
    nd = len(shape)
    return pl.BlockSpec((None,) + shape, lambda *_: (layer,) + (0,) * nd, pipeline_mode=pl.Buffered(1))


def _params(sem, vmem_mib):
    return pltpu.CompilerParams(dimension_semantics=sem, vmem_limit_bytes=vmem_mib * MIB)


def _kv_rows_view(kv5d):
    lead = kv5d.shape[:-3]
    v = kv5d.reshape(lead + (N_MEM, HEADS, HD_TILES, LANES))
    v = jnp.swapaxes(v, -3, -2)
    return v.reshape(lead + (KV_ROWS, LANES))


def _kv_rows_unview(rows4d):
    lead = rows4d.shape[:-2]
    v = rows4d.reshape(lead + (N_MEM, HD_TILES, HEADS, LANES))
    v = jnp.swapaxes(v, -3, -2)
    return v.reshape(lead + (N_MEM, HEADS, HD))


def _head_rows(half, h, n):
    return pl.ds(half * HEADS + h, n, stride=HEADS * HD_TILES)


def _load_head(ref, lead, h):
    parts = [ref[lead + (_head_rows(e, h, N_MEM), slice(None))] for e in range(HD_TILES)]
    return jnp.concatenate(parts, axis=1).astype(BF16)


def _kv_kernel(mem_ref, wk_ref, wv_ref, k_ref, v_ref, wk_sc, wv_sc, *, tm):
    @pl.when(pl.program_id(1) == 0)
    def _():
        wk_sc[...] = wk_ref[...].astype(BF16)
        wv_sc[...] = wv_ref[...].astype(BF16)

    m = mem_ref[...].astype(BF16)
    for w_sc, o_ref in ((wk_sc, k_ref), (wv_sc, v_ref)):
        y = _dot(m, w_sc[...])
        for h in range(HEADS):
            for e in range(HD_TILES):
                c0 = h * HD + e * LANES
                o_ref[0, _head_rows(e, h, tm), :] = y[:, c0:c0 + LANES]


def _mem_kv(mem2d, wk, wv):
    rows = mem2d.shape[0]
    tm = 512
    rpm = HEADS * HD_TILES
    out = jax.ShapeDtypeStruct((DEPTH, rows * rpm, LANES), F32)
    return pl.pallas_call(
        functools.partial(_kv_kernel, tm=tm),
        grid=(DEPTH, rows // tm),
        in_specs=[pl.BlockSpec((tm, D), lambda l, r: (r, 0)),
                  pl.BlockSpec((None, D, D), lambda l, r: (l, 0, 0)),
                  pl.BlockSpec((None, D, D), lambda l, r: (l, 0, 0))],
        out_specs=[pl.BlockSpec((1, tm * rpm, LANES), lambda l, r: (l, r, 0)),
                   pl.BlockSpec((1, tm * rpm, LANES), lambda l, r: (l, r, 0))],
        out_shape=[out, out],
        scratch_shapes=[pltpu.VMEM((D, D), BF16), pltpu.VMEM((D, D), BF16)],
        compiler_params=_params(("arbitrary", "arbitrary"), 40),
        name="mem_kv",
    )(mem2d, wk, wv)


MLP_CH = 1024


def _mlp_kernel(x_ref, g_ref, wup_ref, wdn_ref, gfin_ref, *rest, layer, n_inner, final_norm, emit_bf16):
    if emit_bf16:
        o_ref, wup_b, wdn_b, xn_sc, acc_sc = rest
        wup_b[...] = wup_ref[...].astype(BF16)
        wdn_b[...] = wdn_ref[...].astype(BF16)
    else:
        o_ref, xn_sc, acc_sc = rest
        wup_b, wdn_b = wup_ref, wdn_ref
    c = pl.program_id(1)

    @pl.when(c == 0)
    def _():
        x = x_ref[...]
        xn_sc[...] = _rms(x, g_ref[layer:layer + 1, :]).astype(BF16)
        acc_sc[...] = x

    xn = xn_sc[...]
    acc = acc_sc[...]
    for j in range(n_inner):
        h = _dot(xn, wup_b[:, j * MLP_CH:(j + 1) * MLP_CH])
        h = jnp.maximum(h, 0.0)
        acc = acc + _dot((h * h).astype(BF16), wdn_b[j * MLP_CH:(j + 1) * MLP_CH, :])
    acc_sc[...] = acc

    @pl.when(c == pl.num_programs(1) - 1)
    def _():
        if final_norm:
            o_ref[...] = _rms(acc, gfin_ref[...])
        else:
            o_ref[...] = acc


def _mlp_call(x2d, g, wup, wdn, gfin, layer, w_specs, extra_out_specs, extra_out_shapes, *, tm, hid_steps,
              final_norm):
    rows = x2d.shape[0]
    kern = functools.partial(_mlp_kernel, layer=layer, n_inner=D_FF // hid_steps // MLP_CH, final_norm=final_norm,
                             emit_bf16=bool(extra_out_specs))
    return pl.pallas_call(
        kern,
        grid=(rows // tm, hid_steps),
        in_specs=[pl.BlockSpec((tm, D), lambda r, c: (r, 0)), _const_spec((DEPTH, D))] + w_specs + [
                  _const_spec((1, D))],
        out_specs=[pl.BlockSpec((tm, D), lambda r, c: (r, 0))] + extra_out_specs,
        out_shape=[jax.ShapeDtypeStruct((rows, D), F32)] + extra_out_shapes,
        scratch_shapes=[pltpu.VMEM((tm, D), BF16), pltpu.VMEM((tm, D), F32)],
        compiler_params=_params(("arbitrary", "arbitrary"), 48),
        name="mlp",
    )(x2d, g, wup, wdn, gfin)


def _mlp_resident(x2d, g, wup_b, wdn_b, gfin, layer, *, tm, final_norm):
    w_specs = [_const_spec((D, D_FF)), _const_spec((D_FF, D))]
    return _mlp_call(x2d, g, wup_b, wdn_b, gfin, layer, w_specs, [], [], tm=tm, hid_steps=1,
                     final_norm=final_norm)[0]


def _mlp_streaming(x2d, g, wup, wdn, gfin, layer, *, tm, hid_steps, final_norm):
    hid_blk = D_FF // hid_steps
    w_specs = [pl.BlockSpec((None, D, hid_blk), lambda r, c: (layer, 0, c)),
               pl.BlockSpec((None, hid_blk, D), lambda r, c: (layer, c, 0))]
    b_specs = [pl.BlockSpec((D, hid_blk), lambda r, c: (0, c)), pl.BlockSpec((hid_blk, D), lambda r, c: (c, 0))]
    b_shapes = [jax.ShapeDtypeStruct((D, D_FF), BF16), jax.ShapeDtypeStruct((D_FF, D), BF16)]
    return _mlp_call(x2d, g, wup, wdn, gfin, layer, w_specs, b_specs, b_shapes, tm=tm, hid_steps=hid_steps,
                     final_norm=final_norm)


def _attn_prompt_kernel(x_ref, g_ref, wq_ref, k_ref, v_ref, wo_ref, o_ref, *, layer, tq, sub):
    starts = range(0, tq, sub)
    xs = [x_ref[0, r0:r0 + sub, :] for r0 in starts]
    qs = [(_dot(_rms(x, g_ref[layer:layer + 1, :]).astype(BF16), wq_ref[...]) * ATTN_SCALE).astype(BF16) for x in xs]
    outs = [[] for _ in starts]
    for h in range(HEADS):
        cols = slice(h * HD, (h + 1) * HD)
        kh = _load_head(k_ref, (), h)
        vh = _load_head(v_ref, (), h)
        for q, out in zip(qs, outs):
            s = lax.dot_general(q[:, cols], kh, NT_DIMS, preferred_element_type=F32)
            out.append(_dot(_softmax(s).astype(BF16), vh).astype(BF16))
    for r0, x, out in zip(starts, xs, outs):
        o_ref[0, r0:r0 + sub, :] = x + _dot(jnp.concatenate(out, axis=-1), wo_ref[...])


def _attn_prompt(x3d, g, wq, k4d, v4d, wo, layer, *, tq):
    b, l, _ = x3d.shape
    kv_spec = pl.BlockSpec((None, None, KV_ROWS, LANES), lambda i, t: (layer, i, 0, 0))
    return pl.pallas_call(
        functools.partial(_attn_prompt_kernel, layer=layer, tq=tq, sub=ATTN_SUB),
        grid=(b, l // tq),
        in_specs=[pl.BlockSpec((1, tq, D), lambda i, t: (i, t, 0)),
                  _const_spec((DEPTH, D)),
                  _const_spec((D, D)),
                  kv_spec, kv_spec,
                  _const_spec((D, D))],
        out_specs=pl.BlockSpec((1, tq, D), lambda i, t: (i, t, 0)),
        out_shape=jax.ShapeDtypeStruct(x3d.shape, F32),
        compiler_params=_params(("arbitrary", "arbitrary"), 40),
        name="attn_prompt",
    )(x3d, g, wq, k4d, v4d, wo)


def _attn_sample_kernel(x_ref, g_ref, wq_ref, k_hbm, v_hbm, wo_ref, o_ref, wq_b, wo_b, q_sc, o_sc, kbuf, vbuf, sem,
                        *, layer, sb, seq_len):
    i = pl.program_id(0)
    n_steps = pl.num_programs(0)

    def kv_copies(step, slot):
        seqs = pl.ds(step * sb, sb)
        return (pltpu.make_async_copy(k_hbm.at[layer, seqs], kbuf.at[slot], sem.at[0, slot]),
                pltpu.make_async_copy(v_hbm.at[layer, seqs], vbuf.at[slot], sem.at[1, slot]))

    @pl.when(i == 0)
    def _():
        for step in range(KV_RING - 1):
            for copy in kv_copies(step, step):
                copy.start()

    @pl.when(i + KV_RING - 1 < n_steps)
    def _():
        for copy in kv_copies(i + KV_RING - 1, (i + KV_RING - 1) % KV_RING):
            copy.start()

    @pl.when(i == 0)
    def _():
        wq_b[...] = wq_ref[...].astype(BF16)
        wo_b[...] = wo_ref[...].astype(BF16)
        xn = _rms(x_ref[...], g_ref[layer:layer + 1, :]).astype(BF16)
        q_sc[...] = _dot(xn, wq_b[...]) * ATTN_SCALE

    pair_rows = 2 * seq_len
    first_seq_row = lax.broadcasted_iota(jnp.int32, (pair_rows, HD), 0) < seq_len

    slot = i % KV_RING
    for copy in kv_copies(i, slot):
        copy.wait()
    k_ref = kbuf.at[slot]
    v_ref = vbuf.at[slot]

    for j in range(sb // 2):
        r0 = pl.multiple_of((i * sb + 2 * j) * seq_len, pair_rows)
        q8 = q_sc[pl.ds(r0, pair_rows), :].astype(BF16)
        scores = [lax.dot_general(q8[:, h * HD:(h + 1) * HD], _load_head(k_ref, (2 * j + s,), h), NT_DIMS,
                                  preferred_element_type=F32)
                  for s in range(2) for h in range(HEADS)]
        p = _softmax(jnp.concatenate(scores, axis=0))
        res = []
        for h in range(HEADS):
            oh = []
            for s in range(2):
                blk = (s * HEADS + h) * pair_rows
                oh.append(_dot(p[blk:blk + pair_rows, :].astype(BF16), _load_head(v_ref, (2 * j + s,), h)))
            res.append(jnp.where(first_seq_row, oh[0], oh[1]))
        o_sc[pl.ds(r0, pair_rows), :] = jnp.concatenate(res, axis=-1)

    @pl.when(i == pl.num_programs(0) - 1)
    def _():
        o_ref[...] = x_ref[...] + _dot(o_sc[...].astype(BF16), wo_b[...])


def _attn_sample(x2d, g, wq, k4d, v4d, wo, layer, *, sb, seq_len):
    rows = x2d.shape[0]
    nseq = rows // seq_len
    assert 2 * seq_len == 8 and nseq % sb == 0 and sb % 2 == 0
    assert nseq // sb >= KV_RING
    kern = functools.partial(_attn_sample_kernel, layer=layer, sb=sb, seq_len=seq_len)
    kv_spec = pl.BlockSpec(memory_space=pl.ANY)
    return pl.pallas_call(
        kern,
        grid=(nseq // sb,),
        in_specs=[_const_spec((rows, D)),
                  _const_spec((DEPTH, D)),
                  _layer_spec((D, D), layer),
                  kv_spec, kv_spec,
                  _layer_spec((D, D), layer)],
        out_specs=[pl.BlockSpec((rows, D), lambda i: (0, 0)),
                   pl.BlockSpec((D, D), lambda i: (0, 0)),
                   pl.BlockSpec((D, D), lambda i: (0, 0))],
        out_shape=[jax.ShapeDtypeStruct((rows, D), F32),
                   jax.ShapeDtypeStruct((D, D), BF16),
                   jax.ShapeDtypeStruct((D, D), BF16)],
        scratch_shapes=[pltpu.VMEM((rows, D), F32), pltpu.VMEM((rows, D), F32),
                        pltpu.VMEM((KV_RING, sb, KV_ROWS, LANES), F32),
                        pltpu.VMEM((KV_RING, sb, KV_ROWS, LANES), F32),
                        pltpu.SemaphoreType.DMA((2, KV_RING))],
        compiler_params=_params(("arbitrary",), 58),
        name="attn_sample",
    )(x2d, g, wq, k4d, v4d, wo)


def _glu_in(x, g, win_ref, bin_ref):
    xn = _rms(x, g).astype(BF16)
    h = _dot(xn, win_ref[...]) + bin_ref[...]
    return h[:, :D] * jax.nn.sigmoid(h[:, D:])


def _ln_silu_out(c, lng_ref, lnb_ref, wout_ref):
    mu = jnp.mean(c, axis=-1, keepdims=True)
    cc = c - mu
    var = jnp.mean(cc * cc, axis=-1, keepdims=True)
    n = cc * lax.rsqrt(var + EPS) * lng_ref[...] + lnb_ref[...]
    s = (n * jax.nn.sigmoid(n)).astype(BF16)
    return _dot(s, wout_ref[...])


def _conv_prompt_kernel(x_ref, g_ref, win_ref, bin_ref, dw_ref, dwb_ref, lng_ref, lnb_ref, wout_ref,
                        o_ref, hist_ref, gbuf, cbuf, *, tl, ns):
    n_tiles = D // LANES
    n_groups = tl // 8
    base = HIST_PAD - CONV_BUF

    def rows2(start, n):
        return pl.ds(2 * start, n, stride=2)

    @pl.when(pl.program_id(1) == 0)
    def _():
        for s in range(ns):
            for j in range(n_tiles):
                gbuf[s, j, rows2(0, HIST_PAD), :] = jnp.zeros((HIST_PAD, LANES), F32)

    xs = [x_ref[s] for s in range(ns)]
    xns = [_rms(x, g_ref[0:1, :]).astype(BF16) for x in xs]

    def glu_block(s, c):
        va = slice(c * 256, (c + 1) * 256)
        ga = slice(D + c * 256, D + (c + 1) * 256)
        val = _dot(xns[s], win_ref[:, va]) + bin_ref[:, va]
        gate = _dot(xns[s], win_ref[:, ga]) + bin_ref[:, ga]
        gl = val * jax.nn.sigmoid(gate)
        for jj in range(256 // LANES):
            gbuf[s, 2 * c + jj, rows2(HIST_PAD, tl), :] = gl[:, jj * LANES:(jj + 1) * LANES]

    def conv_tile(s, j):
        lanes = slice(j * LANES, (j + 1) * LANES)
        w = [jnp.broadcast_to(dw_ref[k:k + 1, lanes], (8, LANES)) for k in range(CONV_W)]
        bias = jnp.broadcast_to(dwb_ref[:, lanes], (8, LANES))
        accs = [bias] * 4
        for i in range(n_groups + 3):
            lags = range(max(0, i - n_groups + 1), min(i, 3) + 1)
            for r in range(8):
                taps = [(a, 8 * a + r) for a in lags if 8 * a + r < CONV_W]
                if taps:
                    wnd = gbuf[s, j, rows2(base + 8 * i + r, 8), :]
                    for a, k in taps:
                        accs[a] = accs[a] + w[k] * wnd
            if i >= 3:
                cbuf[s, (i - 3) * 8:(i - 2) * 8, lanes] = accs[3]
            accs = [bias, accs[0], accs[1], accs[2]]

    def finish(s):
        o_ref[s] = xs[s] + _ln_silu_out(cbuf[s], lng_ref, lnb_ref, wout_ref)
        for j in range(n_tiles):
            tail = gbuf[s, j, rows2(tl, HIST_PAD), :]
            hist_ref[s, :, j * LANES:(j + 1) * LANES] = tail
            gbuf[s, j, rows2(0, HIST_PAD), :] = tail

    for c in range(D // 256):
        glu_block(0, c)
    for s in range(ns):
        for j in range(n_tiles):
            conv_tile(s, j)
            if s + 1 < ns and j % 2 == 1:
                glu_block(s + 1, j // 2)
        finish(s)


def _conv_param_specs():
    return [_const_spec((DEPTH, D)),
            _layer_spec((D, 2 * D), 0),
            _const_spec((1, 2 * D)),
            _layer_spec((CONV_W, D), 0),
            _const_spec((1, D)),
            _const_spec((1, D)),
            _const_spec((1, D)),
            _layer_spec((D, D), 0)]


def _conv_prompt(x3d, conv_p, *, tl):
    b, l, _ = x3d.shape
    ns = CONV_NS
    assert b % ns == 0
    kern = functools.partial(_conv_prompt_kernel, tl=tl, ns=ns)
    return pl.pallas_call(
        kern,
        grid=(b // ns, l // tl),
        in_specs=[pl.BlockSpec((ns, tl, D), lambda i, t: (i, t, 0))] + _conv_param_specs(),
        out_specs=[pl.BlockSpec((ns, tl, D), lambda i, t: (i, t, 0)),
                   pl.BlockSpec((ns, HIST_PAD, D), lambda i, t: (i, 0, 0))],
        out_shape=[jax.ShapeDtypeStruct(x3d.shape, F32),
                   jax.ShapeDtypeStruct((b, HIST_PAD, D), F32)],
        scratch_shapes=[pltpu.VMEM((ns, D // LANES, 2 * (HIST_PAD + tl), LANES), F32),
                        pltpu.VMEM((ns, tl, D), F32)],
        compiler_params=_params(("arbitrary", "arbitrary"), 40),
        name="conv_prompt",
    )(x3d, *conv_p)


def _conv_sample_kernel(x_ref, hist_ref, g_ref, win_ref, bin_ref, dw_ref, dwb_ref, lng_ref, lnb_ref, wout_ref,
                        o_ref, nhist_ref, *, nb, seq_len):
    xt = jnp.concatenate([x_ref[:, t * D:(t + 1) * D] for t in range(seq_len)], axis=0)
    gl = _glu_in(xt, g_ref[0:1, :], win_ref, bin_ref)
    g_t = [gl[t * nb:(t + 1) * nb, :] for t in range(seq_len)]

    def padded(r, lanes):
        if r < CONV_BUF:
            return hist_ref[r, :, lanes]
        return g_t[r - CONV_BUF][:, lanes]

    cs = []
    for t in range(seq_len):
        tiles = []
        for j in range(D // LANES):
            lanes = slice(j * LANES, (j + 1) * LANES)
            acc = None
            for k in range(CONV_W):
                term = dw_ref[k:k + 1, lanes] * padded(t + k, lanes)
                acc = term if acc is None else acc + term
            tiles.append(acc)
        cs.append(jnp.concatenate(tiles, axis=-1))
    c = jnp.concatenate(cs, axis=0) + dwb_ref[...]
    res = _ln_silu_out(c, lng_ref, lnb_ref, wout_ref)
    for t in range(seq_len):
        o_ref[:, t * D:(t + 1) * D] = x_ref[:, t * D:(t + 1) * D] + res[t * nb:(t + 1) * nb, :]
    keep = CONV_BUF - seq_len
    nhist_ref[0:keep] = hist_ref[seq_len:CONV_BUF]
    for t in range(seq_len):
        nhist_ref[keep + t] = g_t[t]


def _conv_sample(xv, hist3, conv_p, *, nb, seq_len):
    nseq = xv.shape[0]
    assert seq_len <= CONV_BUF and nseq % nb == 0 and nb % 8 == 0
    kern = functools.partial(_conv_sample_kernel, nb=nb, seq_len=seq_len)
    hist_spec = pl.BlockSpec((CONV_BUF, nb, D), lambda i: (0, i, 0))
    return pl.pallas_call(
        kern,
        grid=(nseq // nb,),
        in_specs=[pl.BlockSpec((nb, seq_len * D), lambda i: (i, 0)), hist_spec] + _conv_param_specs(),
        out_specs=[pl.BlockSpec((nb, seq_len * D), lambda i: (i, 0)), hist_spec],
        out_shape=[jax.ShapeDtypeStruct(xv.shape, F32),
                   jax.ShapeDtypeStruct(hist3.shape, F32)],
        compiler_params=_params(("arbitrary",), 40),
        name="conv_sample",
    )(xv, hist3, *conv_p)


def _s5_scan_chunk_static(hs, st, are_ref, aim_ref, k, tl):
    ar = are_ref[:, k * S5_CW:(k + 1) * S5_CW]
    ai = aim_ref[:, k * S5_CW:(k + 1) * S5_CW]
    cre = slice(2 * k * S5_CW, (2 * k + 1) * S5_CW)
    cim = slice((2 * k + 1) * S5_CW, (2 * k + 2) * S5_CW)
    hr, hi = st[:, cre], st[:, cim]
    for t in range(tl):
        rows = slice(t * 8, (t + 1) * 8)
        hr, hi = ar * hr - ai * hi + hs[rows, cre], ar * hi + ai * hr + hs[rows, cim]
        hs[rows, cre] = hr
        hs[rows, cim] = hi
    st[:, cre] = hr
    st[:, cim] = hi


def _s5_scan(hs, st, are_ref, aim_ref, nb, tl):
    for k in range(S5_CHUNKS):
        ar = are_ref[:, k * S5_CW:(k + 1) * S5_CW]
        ai = aim_ref[:, k * S5_CW:(k + 1) * S5_CW]
        cre = slice(2 * k * S5_CW, (2 * k + 1) * S5_CW)
        cim = slice((2 * k + 1) * S5_CW, (2 * k + 2) * S5_CW)

        def seq_group(bg, carry, ar=ar, ai=ai, cre=cre, cim=cim):
            b0 = pl.multiple_of(bg * 8, 8)

            def step(t, h):
                hr, hi = h
                row = pl.multiple_of(t * nb + b0, 8)
                nr = ar * hr - ai * hi + hs[pl.ds(row, 8), cre]
                ni = ar * hi + ai * hr + hs[pl.ds(row, 8), cim]
                hs[pl.ds(row, 8), cre] = nr
                hs[pl.ds(row, 8), cim] = ni
                return nr, ni

            hr, hi = lax.fori_loop(0, tl, step, (st[pl.ds(b0, 8), cre], st[pl.ds(b0, 8), cim]),
                                   unroll=min(tl, 4))
            st[pl.ds(b0, 8), cre] = hr
            st[pl.ds(b0, 8), cim] = hi
            return carry

        lax.fori_loop(0, nb // 8, seq_group, 0)


def _s5_core(ut, ut_b, bblk_ref, are_ref, aim_ref, cblk_ref, d_ref, hs, st, nb, tl):
    for k in range(S5_CHUNKS):
        hs[:, 2 * k * S5_CW:(2 * k + 2) * S5_CW] = _dot(ut_b[:, k * 256:(k + 1) * 256], bblk_ref[k])
    if nb == 8:
        for k in range(S5_CHUNKS):
            _s5_scan_chunk_static(hs, st, are_ref, aim_ref, k, tl)
    else:
        _s5_scan(hs, st, are_ref, aim_ref, nb, tl)
    ys = [lax.dot_general(hs[:, 2 * k * S5_CW:(2 * k + 2) * S5_CW].astype(BF16), cblk_ref[k], NT_DIMS,
                          preferred_element_type=F32) for k in range(S5_CHUNKS)]
    y = jnp.concatenate(ys, axis=-1) + d_ref[...] * ut
    return jax.nn.gelu(y, approximate=True).astype(BF16)


def _s5_state_in(st, h0re_ref, h0im_ref):
    for k in range(S5_CHUNKS):
        st[:, 2 * k * S5_CW:(2 * k + 1) * S5_CW] = h0re_ref[:, k * S5_CW:(k + 1) * S5_CW]
        st[:, (2 * k + 1) * S5_CW:(2 * k + 2) * S5_CW] = h0im_ref[:, k * S5_CW:(k + 1) * S5_CW]


def _s5_state_out(st, hre_ref, him_ref):
    for k in range(S5_CHUNKS):
        hre_ref[:, k * S5_CW:(k + 1) * S5_CW] = st[:, 2 * k * S5_CW:(2 * k + 1) * S5_CW]
        him_ref[:, k * S5_CW:(k + 1) * S5_CW] = st[:, (2 * k + 1) * S5_CW:(2 * k + 2) * S5_CW]


def _glu_out(z):
    return z[:, :D] * jax.nn.sigmoid(z[:, D:])


def _s5_param_specs():
    return [_const_spec((DEPTH, D)),
            _const_spec((S5_CHUNKS, 256, 2 * S5_CW)),
            _const_spec((8, SSM_LANES)),
            _const_spec((8, SSM_LANES)),
            _const_spec((S5_CHUNKS, 256, 2 * S5_CW)),
            _const_spec((1, D)),
            _layer_spec((D, 2 * D), 0)]


def _s5_prompt_kernel(x_ref, g_ref, bblk_ref, are_ref, aim_ref, cblk_ref, d_ref, wglu_ref,
                      h0re_ref, h0im_ref, o_ref, hre_ref, him_ref, hs, st, ubuf, gtmp, *, nb, tl, n_sub):
    t = pl.program_id(0)
    rows = nb * tl
    n_tiles = D // LANES
    pitch = tl + 8

    @pl.when(t == 0)
    def _():
        _s5_state_in(st, h0re_ref, h0im_ref)

    def chunk_cols(k):
        return slice(2 * k * S5_CW, (2 * k + 2) * S5_CW)

    def c_proj(i, k):
        return lax.dot_general(hs[i, :, chunk_cols(k)].astype(BF16), cblk_ref[k], NT_DIMS,
                               preferred_element_type=F32)

    xs, uts = [], []
    for i in range(n_sub):
        x = x_ref[:, i * tl:(i + 1) * tl, :].reshape(rows, D)
        u = _rms(x, g_ref[1:2, :])
        for j in range(n_tiles):
            for b in range(nb):
                ubuf[i, j, b * pitch:b * pitch + tl, :] = u[b * tl:(b + 1) * tl, j * LANES:(j + 1) * LANES]
        ut = jnp.concatenate(
            [jnp.concatenate([ubuf[i, j, pl.ds(step, nb, stride=pitch), :] for j in range(n_tiles)], axis=1)
             for step in range(tl)], axis=0)
        ut_b = ut.astype(BF16)
        for k in range(S5_CHUNKS):
            hs[i, :, chunk_cols(k)] = _dot(ut_b[:, k * 256:(k + 1) * 256], bblk_ref[k])
        xs.append(x)
        uts.append(ut)

    ys = [[None] * S5_CHUNKS for _ in range(n_sub)]
    for i in range(n_sub):
        for k in range(S5_CHUNKS):
            _s5_scan_chunk_static(hs.at[i], st, are_ref, aim_ref, k, tl)
            if i > 0:
                ys[i - 1][k] = c_proj(i - 1, k)
    for k in range(S5_CHUNKS):
        ys[n_sub - 1][k] = c_proj(n_sub - 1, k)

    for i in range(n_sub):
        y = jnp.concatenate(ys[i], axis=-1) + d_ref[...] * uts[i]
        ge = jax.nn.gelu(y, approximate=True)
        for j in range(n_tiles):
            gtmp[i, j] = ge[:, j * LANES:(j + 1) * LANES]
        ge = jnp.concatenate(
            [jnp.concatenate([gtmp[i, j, pl.ds(b, tl, stride=nb), :] for j in range(n_tiles)], axis=1)
             for b in range(nb)], axis=0).astype(BF16)
        out = _glu_out(_dot(ge, wglu_ref[...]))
        o_ref[:, i * tl:(i + 1) * tl, :] = (xs[i] + out).reshape(nb, tl, D)

    @pl.when(t == pl.num_programs(0) - 1)
    def _():
        _s5_state_out(st, hre_ref, him_ref)


def _s5_prompt(x3d, s5_p, h0re, h0im, *, tl):
    nb, l, _ = x3d.shape
    assert nb % 8 == 0 and nb & (nb - 1) == 0 and tl & (tl - 1) == 0
    rows = nb * tl
    n_sub = S5_SUB
    step = n_sub * tl
    kern = functools.partial(_s5_prompt_kernel, nb=nb, tl=tl, n_sub=n_sub)
    st_shape = jax.ShapeDtypeStruct((nb, SSM_LANES), F32)
    return pl.pallas_call(
        kern,
        grid=(l // step,),
        in_specs=[pl.BlockSpec((nb, step, D), lambda t: (0, t, 0))] + _s5_param_specs() + [
                  _const_spec((nb, SSM_LANES)),
                  _const_spec((nb, SSM_LANES))],
        out_specs=[pl.BlockSpec((nb, step, D), lambda t: (0, t, 0)),
                   pl.BlockSpec((nb, SSM_LANES), lambda t: (0, 0)),
                   pl.BlockSpec((nb, SSM_LANES), lambda t: (0, 0))],
        out_shape=[jax.ShapeDtypeStruct(x3d.shape, F32), st_shape, st_shape],
        scratch_shapes=[pltpu.VMEM((n_sub, rows, 2 * SSM_LANES), F32), pltpu.VMEM((nb, 2 * SSM_LANES), F32),
                        pltpu.VMEM((n_sub, D // LANES, nb * (tl + 8), LANES), F32),
                        pltpu.VMEM((n_sub, D // LANES, rows, LANES), F32)],
        compiler_params=_params(("arbitrary",), 56),
        name="s5_prompt",
    )(x3d, *s5_p, h0re, h0im)


def _s5_sample_kernel(x_ref, g_ref, bblk_ref, are_ref, aim_ref, cblk_ref, d_ref, wglu_ref,
                      h0re_ref, h0im_ref, o_ref, hre_ref, him_ref, hs, st, *, nb, seq_len):
    _s5_state_in(st, h0re_ref, h0im_ref)
    xt = jnp.concatenate([x_ref[:, t * D:(t + 1) * D] for t in range(seq_len)], axis=0)
    ut = _rms(xt, g_ref[1:2, :])
    ge = _s5_core(ut, ut.astype(BF16), bblk_ref, are_ref, aim_ref, cblk_ref, d_ref, hs, st, nb, seq_len)
    out = _glu_out(_dot(ge, wglu_ref[...]))
    for t in range(seq_len):
        o_ref[:, t * D:(t + 1) * D] = x_ref[:, t * D:(t + 1) * D] + out[t * nb:(t + 1) * nb, :]
    _s5_state_out(st, hre_ref, him_ref)


def _s5_sample(xv, s5_p, h0re, h0im, *, nb, seq_len):
    nseq = xv.shape[0]
    assert nseq % nb == 0 and nb % 8 == 0
    rows = nb * seq_len
    kern = functools.partial(_s5_sample_kernel, nb=nb, seq_len=seq_len)
    st_shape = jax.ShapeDtypeStruct((nseq, SSM_LANES), F32)
    st_spec = pl.BlockSpec((nb, SSM_LANES), lambda i: (i, 0))
    return pl.pallas_call(
        kern,
        grid=(nseq // nb,),
        in_specs=[pl.BlockSpec((nb, seq_len * D), lambda i: (i, 0))] + _s5_param_specs() + [st_spec, st_spec],
        out_specs=[pl.BlockSpec((nb, seq_len * D), lambda i: (i, 0)), st_spec, st_spec],
        out_shape=[jax.ShapeDtypeStruct(xv.shape, F32), st_shape, st_shape],
        scratch_shapes=[pltpu.VMEM((rows, 2 * SSM_LANES), F32), pltpu.VMEM((nb, 2 * SSM_LANES), F32)],
        compiler_params=_params(("arbitrary",), 48),
        name="s5_sample",
    )(xv, *s5_p, h0re, h0im)


def _s5_discretize(a_re, a_im, log_dt, b_re, b_im, c_re, c_im):
    dt = jnp.exp(log_dt)[:, None]
    mag = jnp.exp(dt * a_re)
    ab_re = mag * jnp.cos(dt * a_im)
    ab_im = mag * jnp.sin(dt * a_im)
    den = a_re * a_re + a_im * a_im
    num_re = ab_re - 1.0
    coef_re = (num_re * a_re + ab_im * a_im) / den
    coef_im = (ab_im * a_re - num_re * a_im) / den
    bb_re = coef_re[..., None] * b_re - coef_im[..., None] * b_im
    bb_im = coef_re[..., None] * b_im + coef_im[..., None] * b_re
    gpc = SSM_GROUPS // S5_CHUNKS
    cols = 2 * S5_CW
    col_group = (lax.broadcasted_iota(jnp.int32, (gpc, 1, cols), 2) // SSM_STATE) % gpc
    own = col_group == lax.broadcasted_iota(jnp.int32, (gpc, 1, cols), 0)

    def block_diag(compact):
        full = jnp.where(own[None], compact[:, None], 0.0).astype(BF16)
        return full.reshape(S5_CHUNKS, gpc * SSM_GROUP, cols)

    bb = jnp.stack([bb_re, bb_im]).reshape(2, S5_CHUNKS, gpc, SSM_STATE, SSM_GROUP)
    bblk = block_diag(jnp.transpose(bb, (1, 4, 0, 2, 3)).reshape(S5_CHUNKS, SSM_GROUP, cols))
    cc = jnp.stack([c_re, -c_im]).reshape(2, S5_CHUNKS, gpc, SSM_GROUP, SSM_STATE)
    cblk = block_diag(jnp.transpose(cc, (1, 3, 0, 2, 4)).reshape(S5_CHUNKS, SSM_GROUP, cols))
    are = jnp.broadcast_to(ab_re.reshape(1, SSM_LANES), (8, SSM_LANES))
    aim = jnp.broadcast_to(ab_im.reshape(1, SSM_LANES), (8, SSM_LANES))
    return bblk, cblk, are, aim


S5_TL = 32
S5_SUB = 2
CONV_TL = 256
CONV_NS = 1
KV_RING = 3
ATTN_TQ = 1024
ATTN_SUB = 256
MLP_TM = 512


def kernel(x_prompt, x_sample, mem_prompt, cache_conv, state_s5_re, state_s5_im, cache_mem_k, cache_mem_v,
           norm_mix, norm_xattn, norm_ffn, norm_final,
           conv_w_in, conv_b_in, conv_dw, conv_dw_b, conv_ln_g, conv_ln_b, conv_w_out,
           s5_a_re, s5_a_im, s5_log_dt, s5_b_re, s5_b_im, s5_c_re, s5_c_im, s5_d, s5_w_glu,
           xattn_w_q, xattn_w_k, xattn_w_v, xattn_w_o, mlp_w_up, mlp_w_down):
    bp, lp, _ = x_prompt.shape
    bs, ls, _ = x_sample.shape

    conv_p = (norm_mix, conv_w_in.astype(BF16), conv_b_in, conv_dw, conv_dw_b, conv_ln_g, conv_ln_b,
              conv_w_out.astype(BF16))
    bblk, cblk, are, aim = _s5_discretize(s5_a_re[0], s5_a_im[0], s5_log_dt[0], s5_b_re[0], s5_b_im[0],
                                          s5_c_re[0], s5_c_im[0])
    s5_p = (norm_mix, bblk, are, aim, cblk, s5_d, s5_w_glu.astype(BF16))
    gfin = norm_final.reshape(1, D)

    kp, vp = _mem_kv(mem_prompt.reshape(bp * N_MEM, D), xattn_w_k, xattn_w_v)
    kp4 = kp.reshape(DEPTH, bp, KV_ROWS, LANES)
    vp4 = vp.reshape(DEPTH, bp, KV_ROWS, LANES)
    mem_k_prompt = _kv_rows_unview(kp4)
    mem_v_prompt = _kv_rows_unview(vp4)

    def attn_sample(x2d, i):
        return _attn_sample(x2d, norm_xattn, xattn_w_q, ck, cv, xattn_w_o, i, sb=4, seq_len=ls)

    def mlp_sample(x2d, i, final):
        return _mlp_streaming(x2d, norm_ffn, mlp_w_up, mlp_w_down, gfin, i, tm=MLP_TM, hid_steps=4,
                              final_norm=final)

    xp_conv, hist = _conv_prompt(x_prompt, conv_p, tl=CONV_TL)
    (xp_conv, hist), x_sample = lax.optimization_barrier(((xp_conv, hist), x_sample))

    ck = _kv_rows_view(cache_mem_k)
    cv = _kv_rows_view(cache_mem_v)
    hist3 = jnp.transpose(cache_conv[0], (1, 0, 2))
    xv, nhist = _conv_sample(x_sample.reshape(bs, ls * D), hist3, conv_p, nb=32, seq_len=ls)
    conv_sample = jnp.transpose(nhist, (1, 0, 2))[None]
    x, wq0, wo0 = attn_sample(xv.reshape(bs * ls, D), 0)
    x, wup0, wdn0 = mlp_sample(x, 0, False)
    xv, hre, him = _s5_sample(x.reshape(bs, ls * D), s5_p,
                              state_s5_re[0].reshape(bs, SSM_LANES), state_s5_im[0].reshape(bs, SSM_LANES),
                              nb=64, seq_len=ls)
    s5_re_sample = hre.reshape(1, bs, SSM_GROUPS, SSM_STATE)
    s5_im_sample = him.reshape(1, bs, SSM_GROUPS, SSM_STATE)
    x, wq1, wo1 = attn_sample(xv.reshape(bs * ls, D), 1)
    x, wup1, wdn1 = mlp_sample(x, 1, True)
    y_sample = x.reshape(bs, ls, D)

    def mlp_prompt(x3d, wup_b, wdn_b, i, final):
        y = _mlp_resident(x3d.reshape(bp * lp, D), norm_ffn, wup_b, wdn_b, gfin, i, tm=MLP_TM, final_norm=final)
        return y.reshape(bp, lp, D)

    conv_prompt = hist[None, :, HIST_PAD - CONV_BUF:, :]
    x = _attn_prompt(xp_conv, norm_xattn, wq0, kp4, vp4, wo0, 0, tq=ATTN_TQ)
    x = mlp_prompt(x, wup0, wdn0, 0, False)
    zeros_state = jnp.zeros((bp, SSM_LANES), F32)
    x, hre, him = _s5_prompt(x, s5_p, zeros_state, zeros_state, tl=S5_TL)
    s5_re_prompt = hre.reshape(1, bp, SSM_GROUPS, SSM_STATE)
    s5_im_prompt = him.reshape(1, bp, SSM_GROUPS, SSM_STATE)
    x = _attn_prompt(x, norm_xattn, wq1, kp4, vp4, wo1, 1, tq=ATTN_TQ)
    y_prompt = mlp_prompt(x, wup1, wdn1, 1, True)

    return (y_prompt, y_sample, conv_prompt, conv_sample, s5_re_prompt, s5_im_prompt,
            s5_re_sample, s5_im_sample, mem_k_prompt, mem_v_prompt)
```

```python
import functools

import jax
import jax.numpy as jnp
from jax import lax
from jax.experimental import pallas as pl
from jax.experimental.pallas import tpu as pltpu

F32 = jnp.float32
BF16 = jnp.bfloat16

LANES = 128
D = 1024
DEPTH = 2
D_FF = 4 * D
N_MEM = 256
HEADS = 4
HD = D // HEADS
HD_TILES = HD // LANES
KV_ROWS = N_MEM * HEADS * HD_TILES
CONV_W = 31
CONV_BUF = CONV_W - 1
HIST_PAD = 32
SSM_GROUP = 16
SSM_GROUPS = D // SSM_GROUP
SSM_STATE = 64
SSM_LANES = SSM_GROUPS * SSM_STATE
S5_CHUNKS = 4
S5_CW = SSM_LANES // S5_CHUNKS
EPS = 1e-6
ATTN_SCALE = HD ** -0.5
NT_DIMS = (((1,), (1,)), ((), ()))
MIB = 2 ** 20


def _dot(a, b):
    return jnp.dot(a, b, preferred_element_type=F32)


def _rms(x, g):
    ms = jnp.mean(x * x, axis=-1, keepdims=True)
    return x * lax.rsqrt(ms + EPS) * g


def _softmax(s):
    m = jnp.max(s, axis=-1, keepdims=True)
    e = jnp.exp(s - m)
    return e / jnp.sum(e, axis=-1, keepdims=True)


def _const_spec(shape):
    nd = len(shape)
    return pl.BlockSpec(shape, lambda *_: (0,) * nd, pipeline_mode=pl.Buffered(1))


def _layer_spec(shape, layer):
    nd = len(shape)
    return pl.BlockSpec((None,) + shape, lambda *_: (layer,) + (0,) * nd, pipeline_mode=pl.Buffered(1))


def _params(sem, vmem_mib):
    return pltpu.CompilerParams(dimension_semantics=sem, vmem_limit_bytes=vmem_mib * MIB)


def _kv_rows_view(kv5d):
    lead = kv5d.shape[:-3]
    v = kv5d.reshape(lead + (N_MEM, HEADS, HD_TILES, LANES))
    v = jnp.swapaxes(v, -3, -2)
    return v.reshape(lead + (KV_ROWS, LANES))


def _kv_rows_unview(rows4d):
    lead = rows4d.shape[:-2]
    v = rows4d.reshape(lead + (N_MEM, HD_TILES, HEADS, LANES))
    v = jnp.swapaxes(v, -3, -2)
    return v.reshape(lead + (N_MEM, HEADS, HD))


def _head_rows(half, h, n):
    return pl.ds(half * HEADS + h, n, stride=HEADS * HD_TILES)


def _load_head(ref, lead, h):
    parts = [ref[lead + (_head_rows(e, h, N_MEM), slice(None))] for e in range(HD_TILES)]
    return jnp.concatenate(parts, axis=1).astype(BF16)


def _kv_kernel(mem_ref, wk_ref, wv_ref, k_ref, v_ref, wk_sc, wv_sc, *, tm):
    @pl.when(pl.program_id(1) == 0)
    def _():
        wk_sc[...] = wk_ref[...].astype(BF16)
        wv_sc[...] = wv_ref[...].astype(BF16)

    m = mem_ref[...].astype(BF16)
    for w_sc, o_ref in ((wk_sc, k_ref), (wv_sc, v_ref)):
        y = _dot(m, w_sc[...])
        for h in range(HEADS):
            for e in range(HD_TILES):
                c0 = h * HD + e * LANES
                o_ref[0, _head_rows(e, h, tm), :] = y[:, c0:c0 + LANES]


def _mem_kv(mem2d, wk, wv):
    rows = mem2d.shape[0]
    tm = 512
    rpm = HEADS * HD_TILES
    out = jax.ShapeDtypeStruct((DEPTH, rows * rpm, LANES), F32)
    return pl.pallas_call(
        functools.partial(_kv_kernel, tm=tm),
        grid=(DEPTH, rows // tm),
        in_specs=[pl.BlockSpec((tm, D), lambda l, r: (r, 0)),
                  pl.BlockSpec((None, D, D), lambda l, r: (l, 0, 0)),
                  pl.BlockSpec((None, D, D), lambda l, r: (l, 0, 0))],
        out_specs=[pl.BlockSpec((1, tm * rpm, LANES), lambda l, r: (l, r, 0)),
                   pl.BlockSpec((1, tm * rpm, LANES), lambda l, r: (l, r, 0))],
        out_shape=[out, out],
        scratch_shapes=[pltpu.VMEM((D, D), BF16), pltpu.VMEM((D, D), BF16)],
        compiler_params=_params(("arbitrary", "arbitrary"), 40),
        name="mem_kv",
    )(mem2d, wk, wv)


MLP_CH = 1024


def _mlp_kernel(x_ref, g_ref, wup_ref, wdn_ref, gfin_ref, *rest, layer, n_inner, final_norm, emit_bf16):
    if emit_bf16:
        o_ref, wup_b, wdn_b, xn_sc, acc_sc = rest
        wup_b[...] = wup_ref[...].astype(BF16)
        wdn_b[...] = wdn_ref[...].astype(BF16)
    else:
        o_ref, xn_sc, acc_sc = rest
        wup_b, wdn_b = wup_ref, wdn_ref
    c = pl.program_id(1)

    @pl.when(c == 0)
    def _():
        x = x_ref[...]
        xn_sc[...] = _rms(x, g_ref[layer:layer + 1, :]).astype(BF16)
        acc_sc[...] = x

    xn = xn_sc[...]
    acc = acc_sc[...]
    for j in range(n_inner):
        h = _dot(xn, wup_b[:, j * MLP_CH:(j + 1) * MLP_CH])
        h = jnp.maximum(h, 0.0)
        acc = acc + _dot((h * h).astype(BF16), wdn_b[j * MLP_CH:(j + 1) * MLP_CH, :])
    acc_sc[...] = acc

    @pl.when(c == pl.num_programs(1) - 1)
    def _():
        if final_norm:
            o_ref[...] = _rms(acc, gfin_ref[...])
        else:
            o_ref[...] = acc


def _mlp_call(x2d, g, wup, wdn, gfin, layer, w_specs, extra_out_specs, extra_out_shapes, *, tm, hid_steps,
              final_norm):
    rows = x2d.shape[0]
    kern = functools.partial(_mlp_kernel, layer=layer, n_inner=D_FF // hid_steps // MLP_CH, final_norm=final_norm,
                             emit_bf16=bool(extra_out_specs))
    return pl.pallas_call(
        kern,
        grid=(rows // tm, hid_steps),
        in_specs=[pl.BlockSpec((tm, D), lambda r, c: (r, 0)), _const_spec((DEPTH, D))] + w_specs + [
                  _const_spec((1, D))],
        out_specs=[pl.BlockSpec((tm, D), lambda r, c: (r, 0))] + extra_out_specs,
        out_shape=[jax.ShapeDtypeStruct((rows, D), F32)] + extra_out_shapes,
        scratch_shapes=[pltpu.VMEM((tm, D), BF16), pltpu.VMEM((tm, D), F32)],
        compiler_params=_params(("arbitrary", "arbitrary"), 48),
        name="mlp",
    )(x2d, g, wup, wdn, gfin)


def _mlp_resident(x2d, g, wup_b, wdn_b, gfin, layer, *, tm, final_norm):
    w_specs = [_const_spec((D, D_FF)), _const_spec((D_FF, D))]
    return _mlp_call(x2d, g, wup_b, wdn_b, gfin, layer, w_specs, [], [], tm=tm, hid_steps=1,
                     final_norm=final_norm)[0]


def _mlp_streaming(x2d, g, wup, wdn, gfin, layer, *, tm, hid_steps, final_norm):
    hid_blk = D_FF // hid_steps
    w_specs = [pl.BlockSpec((None, D, hid_blk), lambda r, c: (layer, 0, c)),
               pl.BlockSpec((None, hid_blk, D), lambda r, c: (layer, c, 0))]
    b_specs = [pl.BlockSpec((D, hid_blk), lambda r, c: (0, c)), pl.BlockSpec((hid_blk, D), lambda r, c: (c, 0))]
    b_shapes = [jax.ShapeDtypeStruct((D, D_FF), BF16), jax.ShapeDtypeStruct((D_FF, D), BF16)]
    return _mlp_call(x2d, g, wup, wdn, gfin, layer, w_specs, b_specs, b_shapes, tm=tm, hid_steps=hid_steps,
                     final_norm=final_norm)


def _attn_prompt_kernel(x_ref, g_ref, wq_ref, k_ref, v_ref, wo_ref, o_ref, *, layer, tq, sub):
    starts = range(0, tq, sub)
    xs = [x_ref[0, r0:r0 + sub, :] for r0 in starts]
    qs = [(_dot(_rms(x, g_ref[layer:layer + 1, :]).astype(BF16), wq_ref[...]) * ATTN_SCALE).astype(BF16) for x in xs]
    outs = [[] for _ in starts]
    for h in range(HEADS):
        cols = slice(h * HD, (h + 1) * HD)
        kh = _load_head(k_ref, (), h)
        vh = _load_head(v_ref, (), h)
        for q, out in zip(qs, outs):
            s = lax.dot_general(q[:, cols], kh, NT_DIMS, preferred_element_type=F32)
            out.append(_dot(_softmax(s).astype(BF16), vh).astype(BF16))
    for r0, x, out in zip(starts, xs, outs):
        o_ref[0, r0:r0 + sub, :] = x + _dot(jnp.concatenate(out, axis=-1), wo_ref[...])


def _attn_prompt(x3d, g, wq, k4d, v4d, wo, layer, *, tq):
    b, l, _ = x3d.shape
    kv_spec = pl.BlockSpec((None, None, KV_ROWS, LANES), lambda i, t: (layer, i, 0, 0))
    return pl.pallas_call(
        functools.partial(_attn_prompt_kernel, layer=layer, tq=tq, sub=ATTN_SUB),
        grid=(b, l // tq),
        in_specs=[pl.BlockSpec((1, tq, D), lambda i, t: (i, t, 0)),
                  _const_spec((DEPTH, D)),
                  _const_spec((D, D)),
                  kv_spec, kv_spec,
                  _const_spec((D, D))],
        out_specs=pl.BlockSpec((1, tq, D), lambda i, t: (i, t, 0)),
        out_shape=jax.ShapeDtypeStruct(x3d.shape, F32),
        compiler_params=_params(("arbitrary", "arbitrary"), 40),
        name="attn_prompt",
    )(x3d, g, wq, k4d, v4d, wo)


def _attn_sample_kernel(x_ref, g_ref, wq_ref, k_hbm, v_hbm, wo_ref, o_ref, wq_b, wo_b, q_sc, o_sc, kbuf, vbuf, sem,
                        *, layer, sb, seq_len):
    i = pl.program_id(0)
    n_steps = pl.num_programs(0)

    def kv_copies(step, slot):
        seqs = pl.ds(step * sb, sb)
        return (pltpu.make_async_copy(k_hbm.at[layer, seqs], kbuf.at[slot], sem.at[0, slot]),
                pltpu.make_async_copy(v_hbm.at[layer, seqs], vbuf.at[slot], sem.at[1, slot]))

    @pl.when(i == 0)
    def _():
        for step in range(KV_RING - 1):
            for copy in kv_copies(step, step):
                copy.start()

    @pl.when(i + KV_RING - 1 < n_steps)
    def _():
        for copy in kv_copies(i + KV_RING - 1, (i + KV_RING - 1) % KV_RING):
            copy.start()

    @pl.when(i == 0)
    def _():
        wq_b[...] = wq_ref[...].astype(BF16)
        wo_b[...] = wo_ref[...].astype(BF16)
        xn = _rms(x_ref[...], g_ref[layer:layer + 1, :]).astype(BF16)
        q_sc[...] = _dot(xn, wq_b[...]) * ATTN_SCALE

    pair_rows = 2 * seq_len
    first_seq_row = lax.broadcasted_iota(jnp.int32, (pair_rows, HD), 0) < seq_len

    slot = i % KV_RING
    for copy in kv_copies(i, slot):
        copy.wait()
    k_ref = kbuf.at[slot]
    v_ref = vbuf.at[slot]

    for j in range(sb // 2):
        r0 = pl.multiple_of((i * sb + 2 * j) * seq_len, pair_rows)
        q8 = q_sc[pl.ds(r0, pair_rows), :].astype(BF16)
        scores = [lax.dot_general(q8[:, h * HD:(h + 1) * HD], _load_head(k_ref, (2 * j + s,), h), NT_DIMS,
                                  preferred_element_type=F32)
                  for s in range(2) for h in range(HEADS)]
        p = _softmax(jnp.concatenate(scores, axis=0))
        res = []
        for h in range(HEADS):
            oh = []
            for s in range(2):
                blk = (s * HEADS + h) * pair_rows
                oh.append(_dot(p[blk:blk + pair_rows, :].astype(BF16), _load_head(v_ref, (2 * j + s,), h)))
            res.append(jnp.where(first_seq_row, oh[0], oh[1]))
        o_sc[pl.ds(r0, pair_rows), :] = jnp.concatenate(res, axis=-1)

    @pl.when(i == pl.num_programs(0) - 1)
    def _():
        o_ref[...] = x_ref[...] + _dot(o_sc[...].astype(BF16), wo_b[...])


def _attn_sample(x2d, g, wq, k4d, v4d, wo, layer, *, sb, seq_len):
    rows = x2d.shape[0]
    nseq = rows // seq_len
    assert 2 * seq_len == 8 and nseq % sb == 0 and sb % 2 == 0
    assert nseq // sb >= KV_RING
    kern = functools.partial(_attn_sample_kernel, layer=layer, sb=sb, seq_len=seq_len)
    kv_spec = pl.BlockSpec(memory_space=pl.ANY)
    return pl.pallas_call(
        kern,
        grid=(nseq // sb,),
        in_specs=[_const_spec((rows, D)),
                  _const_spec((DEPTH, D)),
                  _layer_spec((D, D), layer),
                  kv_spec, kv_spec,
                  _layer_spec((D, D), layer)],
        out_specs=[pl.BlockSpec((rows, D), lambda i: (0, 0)),
                   pl.BlockSpec((D, D), lambda i: (0, 0)),
                   pl.BlockSpec((D, D), lambda i: (0, 0))],
        out_shape=[jax.ShapeDtypeStruct((rows, D), F32),
                   jax.ShapeDtypeStruct((D, D), BF16),
                   jax.ShapeDtypeStruct((D, D), BF16)],
        scratch_shapes=[pltpu.VMEM((rows, D), F32), pltpu.VMEM((rows, D), F32),
                        pltpu.VMEM((KV_RING, sb, KV_ROWS, LANES), F32),
                        pltpu.VMEM((KV_RING, sb, KV_ROWS, LANES), F32),
                        pltpu.SemaphoreType.DMA((2, KV_RING))],
        compiler_params=_params(("arbitrary",), 58),
        name="attn_sample",
    )(x2d, g, wq, k4d, v4d, wo)


def _glu_in(x, g, win_ref, bin_ref):
    xn = _rms(x, g).astype(BF16)
    h = _dot(xn, win_ref[...]) + bin_ref[...]
    return h[:, :D] * jax.nn.sigmoid(h[:, D:])


def _ln_silu_out(c, lng_ref, lnb_ref, wout_ref):
    mu = jnp.mean(c, axis=-1, keepdims=True)
    cc = c - mu
    var = jnp.mean(cc * cc, axis=-1, keepdims=True)
    n = cc * lax.rsqrt(var + EPS) * lng_ref[...] + lnb_ref[...]
    s = (n * jax.nn.sigmoid(n)).astype(BF16)
    return _dot(s, wout_ref[...])


def _conv_prompt_kernel(x_ref, g_ref, win_ref, bin_ref, dw_ref, dwb_ref, lng_ref, lnb_ref, wout_ref,
                        o_ref, hist_ref, gbuf, cbuf, *, tl, ns):
    n_tiles = D // LANES
    n_groups = tl // 8
    base = HIST_PAD - CONV_BUF

    def rows2(start, n):
        return pl.ds(2 * start, n, stride=2)

    @pl.when(pl.program_id(1) == 0)
    def _():
        for s in range(ns):
            for j in range(n_tiles):
                gbuf[s, j, rows2(0, HIST_PAD), :] = jnp.zeros((HIST_PAD, LANES), F32)

    xs = [x_ref[s] for s in range(ns)]
    xns = [_rms(x, g_ref[0:1, :]).astype(BF16) for x in xs]

    def glu_block(s, c):
        va = slice(c * 256, (c + 1) * 256)
        ga = slice(D + c * 256, D + (c + 1) * 256)
        val = _dot(xns[s], win_ref[:, va]) + bin_ref[:, va]
        gate = _dot(xns[s], win_ref[:, ga]) + bin_ref[:, ga]
        gl = val * jax.nn.sigmoid(gate)
        for jj in range(256 // LANES):
            gbuf[s, 2 * c + jj, rows2(HIST_PAD, tl), :] = gl[:, jj * LANES:(jj + 1) * LANES]

    def conv_tile(s, j):
        lanes = slice(j * LANES, (j + 1) * LANES)
        w = [jnp.broadcast_to(dw_ref[k:k + 1, lanes], (8, LANES)) for k in range(CONV_W)]
        bias = jnp.broadcast_to(dwb_ref[:, lanes], (8, LANES))
        accs = [bias] * 4
        for i in range(n_groups + 3):
            lags = range(max(0, i - n_groups + 1), min(i, 3) + 1)
            for r in range(8):
                taps = [(a, 8 * a + r) for a in lags if 8 * a + r < CONV_W]
                if taps:
                    wnd = gbuf[s, j, rows2(base + 8 * i + r, 8), :]
                    for a, k in taps:
                        accs[a] = accs[a] + w[k] * wnd
            if i >= 3:
                cbuf[s, (i - 3) * 8:(i - 2) * 8, lanes] = accs[3]
            accs = [bias, accs[0], accs[1], accs[2]]

    def finish(s):
        o_ref[s] = xs[s] + _ln_silu_out(cbuf[s], lng_ref, lnb_ref, wout_ref)
        for j in range(n_tiles):
            tail = gbuf[s, j, rows2(tl, HIST_PAD), :]
            hist_ref[s, :, j * LANES:(j + 1) * LANES] = tail
            gbuf[s, j, rows2(0, HIST_PAD), :] = tail

    for c in range(D // 256):
        glu_block(0, c)
    for s in range(ns):
        for j in range(n_tiles):
            conv_tile(s, j)
            if s + 1 < ns and j % 2 == 1:
                glu_block(s + 1, j // 2)
        finish(s)


def _conv_param_specs():
    return [_const_spec((DEPTH, D)),
            _layer_spec((D, 2 * D), 0),
            _const_spec((1, 2 * D)),
            _layer_spec((CONV_W, D), 0),
            _const_spec((1, D)),
            _const_spec((1, D)),
            _const_spec((1, D)),
            _layer_spec((D, D), 0)]


def _conv_prompt(x3d, conv_p, *, tl):
    b, l, _ = x3d.shape
    ns = CONV_NS
    assert b % ns == 0
    kern = functools.partial(_conv_prompt_kernel, tl=tl, ns=ns)
    return pl.pallas_call(
        kern,
        grid=(b // ns, l // tl),
        in_specs=[pl.BlockSpec((ns, tl, D), lambda i, t: (i, t, 0))] + _conv_param_specs(),
        out_specs=[pl.BlockSpec((ns, tl, D), lambda i, t: (i, t, 0)),
                   pl.BlockSpec((ns, HIST_PAD, D), lambda i, t: (i, 0, 0))],
        out_shape=[jax.ShapeDtypeStruct(x3d.shape, F32),
                   jax.ShapeDtypeStruct((b, HIST_PAD, D), F32)],
        scratch_shapes=[pltpu.VMEM((ns, D // LANES, 2 * (HIST_PAD + tl), LANES), F32),
                        pltpu.VMEM((ns, tl, D), F32)],
        compiler_params=_params(("arbitrary", "arbitrary"), 40),
        name="conv_prompt",
    )(x3d, *conv_p)


def _conv_sample_kernel(x_ref, hist_ref, g_ref, win_ref, bin_ref, dw_ref, dwb_ref, lng_ref, lnb_ref, wout_ref,
                        o_ref, nhist_ref, *, nb, seq_len):
    xt = jnp.concatenate([x_ref[:, t * D:(t + 1) * D] for t in range(seq_len)], axis=0)
    gl = _glu_in(xt, g_ref[0:1, :], win_ref, bin_ref)
    g_t = [gl[t * nb:(t + 1) * nb, :] for t in range(seq_len)]

    def padded(r, lanes):
        if r < CONV_BUF:
            return hist_ref[r, :, lanes]
        return g_t[r - CONV_BUF][:, lanes]

    cs = []
    for t in range(seq_len):
        tiles = []
        for j in range(D // LANES):
            lanes = slice(j * LANES, (j + 1) * LANES)
            acc = None
            for k in range(CONV_W):
                term = dw_ref[k:k + 1, lanes] * padded(t + k, lanes)
                acc = term if acc is None else acc + term
            tiles.append(acc)
        cs.append(jnp.concatenate(tiles, axis=-1))
    c = jnp.concatenate(cs, axis=0) + dwb_ref[...]
    res = _ln_silu_out(c, lng_ref, lnb_ref, wout_ref)
    for t in range(seq_len):
        o_ref[:, t * D:(t + 1) * D] = x_ref[:, t * D:(t + 1) * D] + res[t * nb:(t + 1) * nb, :]
    keep = CONV_BUF - seq_len
    nhist_ref[0:keep] = hist_ref[seq_len:CONV_BUF]
    for t in range(seq_len):
        nhist_ref[keep + t] = g_t[t]


def _conv_sample(xv, hist3, conv_p, *, nb, seq_len):
    nseq = xv.shape[0]
    assert seq_len <= CONV_BUF and nseq % nb == 0 and nb % 8 == 0
    kern = functools.partial(_conv_sample_kernel, nb=nb, seq_len=seq_len)
    hist_spec = pl.BlockSpec((CONV_BUF, nb, D), lambda i: (0, i, 0))
    return pl.pallas_call(
        kern,
        grid=(nseq // nb,),
        in_specs=[pl.BlockSpec((nb, seq_len * D), lambda i: (i, 0)), hist_spec] + _conv_param_specs(),
        out_specs=[pl.BlockSpec((nb, seq_len * D), lambda i: (i, 0)), hist_spec],
        out_shape=[jax.ShapeDtypeStruct(xv.shape, F32),
                   jax.ShapeDtypeStruct(hist3.shape, F32)],
        compiler_params=_params(("arbitrary",), 40),
        name="conv_sample",
    )(xv, hist3, *conv_p)


def _s5_scan_chunk_static(hs, st, are_ref, aim_ref, k, tl):
    ar = are_ref[:, k * S5_CW:(k + 1) * S5_CW]
    ai = aim_ref[:, k * S5_CW:(k + 1) * S5_CW]
    cre = slice(2 * k * S5_CW, (2 * k + 1) * S5_CW)
    cim = slice((2 * k + 1) * S5_CW, (2 * k + 2) * S5_CW)
    hr, hi = st[:, cre], st[:, cim]
    for t in range(tl):
        rows = slice(t * 8, (t + 1) * 8)
        hr, hi = ar * hr - ai * hi + hs[rows, cre], ar * hi + ai * hr + hs[rows, cim]
        hs[rows, cre] = hr
        hs[rows, cim] = hi
    st[:, cre] = hr
    st[:, cim] = hi


def _s5_scan(hs, st, are_ref, aim_ref, nb, tl):
    for k in range(S5_CHUNKS):
        ar = are_ref[:, k * S5_CW:(k + 1) * S5_CW]
        ai = aim_ref[:, k * S5_CW:(k + 1) * S5_CW]
        cre = slice(2 * k * S5_CW, (2 * k + 1) * S5_CW)
        cim = slice((2 * k + 1) * S5_CW, (2 * k + 2) * S5_CW)

        def seq_group(bg, carry, ar=ar, ai=ai, cre=cre, cim=cim):
            b0 = pl.multiple_of(bg * 8, 8)

            def step(t, h):
                hr, hi = h
                row = pl.multiple_of(t * nb + b0, 8)
                nr = ar * hr - ai * hi + hs[pl.ds(row, 8), cre]
                ni = ar * hi + ai * hr + hs[pl.ds(row, 8), cim]
                hs[pl.ds(row, 8), cre] = nr
                hs[pl.ds(row, 8), cim] = ni
                return nr, ni

            hr, hi = lax.fori_loop(0, tl, step, (st[pl.ds(b0, 8), cre], st[pl.ds(b0, 8), cim]),
                                   unroll=min(tl, 4))
            st[pl.ds(b0, 8), cre] = hr
            st[pl.ds(b0, 8), cim] = hi
            return carry

        lax.fori_loop(0, nb // 8, seq_group, 0)


def _s5_core(ut, ut_b, bblk_ref, are_ref, aim_ref, cblk_ref, d_ref, hs, st, nb, tl):
    for k in range(S5_CHUNKS):
        hs[:, 2 * k * S5_CW:(2 * k + 2) * S5_CW] = _dot(ut_b[:, k * 256:(k + 1) * 256], bblk_ref[k])
    if nb == 8:
        for k in range(S5_CHUNKS):
            _s5_scan_chunk_static(hs, st, are_ref, aim_ref, k, tl)
    else:
        _s5_scan(hs, st, are_ref, aim_ref, nb, tl)
    ys = [lax.dot_general(hs[:, 2 * k * S5_CW:(2 * k + 2) * S5_CW].astype(BF16), cblk_ref[k], NT_DIMS,
                          preferred_element_type=F32) for k in range(S5_CHUNKS)]
    y = jnp.concatenate(ys, axis=-1) + d_ref[...] * ut
    return jax.nn.gelu(y, approximate=True).astype(BF16)


def _s5_state_in(st, h0re_ref, h0im_ref):
    for k in range(S5_CHUNKS):
        st[:, 2 * k * S5_CW:(2 * k + 1) * S5_CW] = h0re_ref[:, k * S5_CW:(k + 1) * S5_CW]
        st[:, (2 * k + 1) * S5_CW:(2 * k + 2) * S5_CW] = h0im_ref[:, k * S5_CW:(k + 1) * S5_CW]


def _s5_state_out(st, hre_ref, him_ref):
    for k in range(S5_CHUNKS):
        hre_ref[:, k * S5_CW:(k + 1) * S5_CW] = st[:, 2 * k * S5_CW:(2 * k + 1) * S5_CW]
        him_ref[:, k * S5_CW:(k + 1) * S5_CW] = st[:, (2 * k + 1) * S5_CW:(2 * k + 2) * S5_CW]


def _glu_out(z):
    return z[:, :D] * jax.nn.sigmoid(z[:, D:])


def _s5_param_specs():
    return [_const_spec((DEPTH, D)),
            _const_spec((S5_CHUNKS, 256, 2 * S5_CW)),
            _const_spec((8, SSM_LANES)),
            _const_spec((8, SSM_LANES)),
            _const_spec((S5_CHUNKS, 256, 2 * S5_CW)),
            _const_spec((1, D)),
            _layer_spec((D, 2 * D), 0)]


def _s5_prompt_kernel(x_ref, g_ref, bblk_ref, are_ref, aim_ref, cblk_ref, d_ref, wglu_ref,
                      h0re_ref, h0im_ref, o_ref, hre_ref, him_ref, hs, st, ubuf, gtmp, *, nb, tl, n_sub):
    t = pl.program_id(0)
    rows = nb * tl
    n_tiles = D // LANES
    pitch = tl + 8

    @pl.when(t == 0)
    def _():
        _s5_state_in(st, h0re_ref, h0im_ref)

    def chunk_cols(k):
        return slice(2 * k * S5_CW, (2 * k + 2) * S5_CW)

    def c_proj(i, k):
        return lax.dot_general(hs[i, :, chunk_cols(k)].astype(BF16), cblk_ref[k], NT_DIMS,
                               preferred_element_type=F32)

    xs, uts = [], []
    for i in range(n_sub):
        x = x_ref[:, i * tl:(i + 1) * tl, :].reshape(rows, D)
        u = _rms(x, g_ref[1:2, :])
        for j in range(n_tiles):
            for b in range(nb):
                ubuf[i, j, b * pitch:b * pitch + tl, :] = u[b * tl:(b + 1) * tl, j * LANES:(j + 1) * LANES]
        ut = jnp.concatenate(
            [jnp.concatenate([ubuf[i, j, pl.ds(step, nb, stride=pitch), :] for j in range(n_tiles)], axis=1)
             for step in range(tl)], axis=0)
        ut_b = ut.astype(BF16)
        for k in range(S5_CHUNKS):
            hs[i, :, chunk_cols(k)] = _dot(ut_b[:, k * 256:(k + 1) * 256], bblk_ref[k])
        xs.append(x)
        uts.append(ut)

    ys = [[None] * S5_CHUNKS for _ in range(n_sub)]
    for i in range(n_sub):
        for k in range(S5_CHUNKS):
            _s5_scan_chunk_static(hs.at[i], st, are_ref, aim_ref, k, tl)
            if i > 0:
                ys[i - 1][k] = c_proj(i - 1, k)
    for k in range(S5_CHUNKS):
        ys[n_sub - 1][k] = c_proj(n_sub - 1, k)

    for i in range(n_sub):
        y = jnp.concatenate(ys[i], axis=-1) + d_ref[...] * uts[i]
        ge = jax.nn.gelu(y, approximate=True)
        for j in range(n_tiles):
            gtmp[i, j] = ge[:, j * LANES:(j + 1) * LANES]
        ge = jnp.concatenate(
            [jnp.concatenate([gtmp[i, j, pl.ds(b, tl, stride=nb), :] for j in range(n_tiles)], axis=1)
             for b in range(nb)], axis=0).astype(BF16)
        out = _glu_out(_dot(ge, wglu_ref[...]))
        o_ref[:, i * tl:(i + 1) * tl, :] = (xs[i] + out).reshape(nb, tl, D)

    @pl.when(t == pl.num_programs(0) - 1)
    def _():
        _s5_state_out(st, hre_ref, him_ref)


def _s5_prompt(x3d, s5_p, h0re, h0im, *, tl):
    nb, l, _ = x3d.shape
    assert nb % 8 == 0 and nb & (nb - 1) == 0 and tl & (tl - 1) == 0
    rows = nb * tl
    n_sub = S5_SUB
    step = n_sub * tl
    kern = functools.partial(_s5_prompt_kernel, nb=nb, tl=tl, n_sub=n_sub)
    st_shape = jax.ShapeDtypeStruct((nb, SSM_LANES), F32)
    return pl.pallas_call(
        kern,
        grid=(l // step,),
        in_specs=[pl.BlockSpec((nb, step, D), lambda t: (0, t, 0))] + _s5_param_specs() + [
                  _const_spec((nb, SSM_LANES)),
                  _const_spec((nb, SSM_LANES))],
        out_specs=[pl.BlockSpec((nb, step, D), lambda t: (0, t, 0)),
                   pl.BlockSpec((nb, SSM_LANES), lambda t: (0, 0)),
                   pl.BlockSpec((nb, SSM_LANES), lambda t: (0, 0))],
        out_shape=[jax.ShapeDtypeStruct(x3d.shape, F32), st_shape, st_shape],
        scratch_shapes=[pltpu.VMEM((n_sub, rows, 2 * SSM_LANES), F32), pltpu.VMEM((nb, 2 * SSM_LANES), F32),
                        pltpu.VMEM((n_sub, D // LANES, nb * (tl + 8), LANES), F32),
                        pltpu.VMEM((n_sub, D // LANES, rows, LANES), F32)],
        compiler_params=_params(("arbitrary",), 56),
        name="s5_prompt",
    )(x3d, *s5_p, h0re, h0im)


def _s5_sample_kernel(x_ref, g_ref, bblk_ref, are_ref, aim_ref, cblk_ref, d_ref, wglu_ref,
                      h0re_ref, h0im_ref, o_ref, hre_ref, him_ref, hs, st, *, nb, seq_len):
    _s5_state_in(st, h0re_ref, h0im_ref)
    xt = jnp.concatenate([x_ref[:, t * D:(t + 1) * D] for t in range(seq_len)], axis=0)
    ut = _rms(xt, g_ref[1:2, :])
    ge = _s5_core(ut, ut.astype(BF16), bblk_ref, are_ref, aim_ref, cblk_ref, d_ref, hs, st, nb, seq_len)
    out = _glu_out(_dot(ge, wglu_ref[...]))
    for t in range(seq_len):
        o_ref[:, t * D:(t + 1) * D] = x_ref[:, t * D:(t + 1) * D] + out[t * nb:(t + 1) * nb, :]
    _s5_state_out(st, hre_ref, him_ref)


def _s5_sample(xv, s5_p, h0re, h0im, *, nb, seq_len):
    nseq = xv.shape[0]
    assert nseq % nb == 0 and nb % 8 == 0
    rows = nb * seq_len
    kern = functools.partial(_s5_sample_kernel, nb=nb, seq_len=seq_len)
    st_shape = jax.ShapeDtypeStruct((nseq, SSM_LANES), F32)
    st_spec = pl.BlockSpec((nb, SSM_LANES), lambda i: (i, 0))
    return pl.pallas_call(
        kern,
        grid=(nseq // nb,),
        in_specs=[pl.BlockSpec((nb, seq_len * D), lambda i: (i, 0))] + _s5_param_specs() + [st_spec, st_spec],
        out_specs=[pl.BlockSpec((nb, seq_len * D), lambda i: (i, 0)), st_spec, st_spec],
        out_shape=[jax.ShapeDtypeStruct(xv.shape, F32), st_shape, st_shape],
        scratch_shapes=[pltpu.VMEM((rows, 2 * SSM_LANES), F32), pltpu.VMEM((nb, 2 * SSM_LANES), F32)],
        compiler_params=_params(("arbitrary",), 48),
        name="s5_sample",
    )(xv, *s5_p, h0re, h0im)


def _s5_discretize(a_re, a_im, log_dt, b_re, b_im, c_re, c_im):
    dt = jnp.exp(log_dt)[:, None]
    mag = jnp.exp(dt * a_re)
    ab_re = mag * jnp.cos(dt * a_im)
    ab_im = mag * jnp.sin(dt * a_im)
    den = a_re * a_re + a_im * a_im
    num_re = ab_re - 1.0
    coef_re = (num_re * a_re + ab_im * a_im) / den
    coef_im = (ab_im * a_re - num_re * a_im) / den
    bb_re = coef_re[..., None] * b_re - coef_im[..., None] * b_im
    bb_im = coef_re[..., None] * b_im + coef_im[..., None] * b_re
    gpc = SSM_GROUPS // S5_CHUNKS
    cols = 2 * S5_CW
    col_group = (lax.broadcasted_iota(jnp.int32, (gpc, 1, cols), 2) // SSM_STATE) % gpc
    own = col_group == lax.broadcasted_iota(jnp.int32, (gpc, 1, cols), 0)

    def block_diag(compact):
        full = jnp.where(own[None], compact[:, None], 0.0).astype(BF16)
        return full.reshape(S5_CHUNKS, gpc * SSM_GROUP, cols)

    bb = jnp.stack([bb_re, bb_im]).reshape(2, S5_CHUNKS, gpc, SSM_STATE, SSM_GROUP)
    bblk = block_diag(jnp.transpose(bb, (1, 4, 0, 2, 3)).reshape(S5_CHUNKS, SSM_GROUP, cols))
    cc = jnp.stack([c_re, -c_im]).reshape(2, S5_CHUNKS, gpc, SSM_GROUP, SSM_STATE)
    cblk = block_diag(jnp.transpose(cc, (1, 3, 0, 2, 4)).reshape(S5_CHUNKS, SSM_GROUP, cols))
    are = jnp.broadcast_to(ab_re.reshape(1, SSM_LANES), (8, SSM_LANES))
    aim = jnp.broadcast_to(ab_im.reshape(1, SSM_LANES), (8, SSM_LANES))
    return bblk, cblk, are, aim


S5_TL = 32
S5_SUB = 2
CONV_TL = 512
CONV_NS = 1
KV_RING = 3
ATTN_TQ = 1024
ATTN_SUB = 256
MLP_TM = 1024


def kernel(x_prompt, x_sample, mem_prompt, cache_conv, state_s5_re, state_s5_im, cache_mem_k, cache_mem_v,
           norm_mix, norm_xattn, norm_ffn, norm_final,
           conv_w_in, conv_b_in, conv_dw, conv_dw_b, conv_ln_g, conv_ln_b, conv_w_out,
           s5_a_re, s5_a_im, s5_log_dt, s5_b_re, s5_b_im, s5_c_re, s5_c_im, s5_d, s5_w_glu,
           xattn_w_q, xattn_w_k, xattn_w_v, xattn_w_o, mlp_w_up, mlp_w_down):
    bp, lp, _ = x_prompt.shape
    bs, ls, _ = x_sample.shape

    conv_p = (norm_mix, conv_w_in.astype(BF16), conv_b_in, conv_dw, conv_dw_b, conv_ln_g, conv_ln_b,
              conv_w_out.astype(BF16))
    bblk, cblk, are, aim = _s5_discretize(s5_a_re[0], s5_a_im[0], s5_log_dt[0], s5_b_re[0], s5_b_im[0],
                                          s5_c_re[0], s5_c_im[0])
    s5_p = (norm_mix, bblk, are, aim, cblk, s5_d, s5_w_glu.astype(BF16))
    gfin = norm_final.reshape(1, D)

    kp, vp = _mem_kv(mem_prompt.reshape(bp * N_MEM, D), xattn_w_k, xattn_w_v)
    kp4 = kp.reshape(DEPTH, bp, KV_ROWS, LANES)
    vp4 = vp.reshape(DEPTH, bp, KV_ROWS, LANES)
    mem_k_prompt = _kv_rows_unview(kp4)
    mem_v_prompt = _kv_rows_unview(vp4)

    def attn_sample(x2d, i):
        return _attn_sample(x2d, norm_xattn, xattn_w_q, ck, cv, xattn_w_o, i, sb=4, seq_len=ls)

    def mlp_sample(x2d, i, final):
        return _mlp_streaming(x2d, norm_ffn, mlp_w_up, mlp_w_down, gfin, i, tm=min(MLP_TM, bs * ls), hid_steps=4,
                              final_norm=final)

    xp_conv, hist = _conv_prompt(x_prompt, conv_p, tl=CONV_TL)
    (xp_conv, hist), x_sample = lax.optimization_barrier(((xp_conv, hist), x_sample))

    ck = _kv_rows_view(cache_mem_k)
    cv = _kv_rows_view(cache_mem_v)
    hist3 = jnp.transpose(cache_conv[0], (1, 0, 2))
    xv, nhist = _conv_sample(x_sample.reshape(bs, ls * D), hist3, conv_p, nb=32, seq_len=ls)
    conv_sample = jnp.transpose(nhist, (1, 0, 2))[None]
    x, wq0, wo0 = attn_sample(xv.reshape(bs * ls, D), 0)
    x, wup0, wdn0 = mlp_sample(x, 0, False)
    xv, hre, him = _s5_sample(x.reshape(bs, ls * D), s5_p,
                              state_s5_re[0].reshape(bs, SSM_LANES), state_s5_im[0].reshape(bs, SSM_LANES),
                              nb=64, seq_len=ls)
    s5_re_sample = hre.reshape(1, bs, SSM_GROUPS, SSM_STATE)
    s5_im_sample = him.reshape(1, bs, SSM_GROUPS, SSM_STATE)
    x, wq1, wo1 = attn_sample(xv.reshape(bs * ls, D), 1)
    x, wup1, wdn1 = mlp_sample(x, 1, True)
    y_sample = x.reshape(bs, ls, D)

    def mlp_prompt(x3d, wup_b, wdn_b, i, final):
        y = _mlp_resident(x3d.reshape(bp * lp, D), norm_ffn, wup_b, wdn_b, gfin, i, tm=MLP_TM, final_norm=final)
        return y.reshape(bp, lp, D)

    conv_prompt = hist[None, :, HIST_PAD - CONV_BUF:, :]
    x = _attn_prompt(xp_conv, norm_xattn, wq0, kp4, vp4, wo0, 0, tq=ATTN_TQ)
    x = mlp_prompt(x, wup0, wdn0, 0, False)
    zeros_state = jnp.zeros((bp, SSM_LANES), F32)
    x, hre, him = _s5_prompt(x, s5_p, zeros_state, zeros_state, tl=S5_TL)
    s5_re_prompt = hre.reshape(1, bp, SSM_GROUPS, SSM_STATE)
    s5_im_prompt = him.reshape(1, bp, SSM_GROUPS, SSM_STATE)
    x = _attn_prompt(x, norm_xattn, wq1, kp4, vp4, wo1, 1, tq=ATTN_TQ)
    y_prompt = mlp_prompt(x, wup1, wdn1, 1, True)

    return (y_prompt, y_sample, conv_prompt, conv_sample, s5_re_prompt, s5_im_prompt,
            s5_re_sample, s5_im_sample, mem_k_prompt, mem_v_prompt)
```

```python
import functools

import jax
import jax.numpy as jnp
from jax import lax
from jax.experimental import pallas as pl
from jax.experimental.pallas import tpu as pltpu

F32 = jnp.float32
BF16 = jnp.bfloat16

LANES = 128
D = 1024
DEPTH = 2
D_FF = 4 * D
N_MEM = 256
HEADS = 4
HD = D // HEADS
HD_TILES = HD // LANES
KV_ROWS = N_MEM * HEADS * HD_TILES
CONV_W = 31
CONV_BUF = CONV_W - 1
HIST_PAD = 32
SSM_GROUP = 16
SSM_GROUPS = D // SSM_GROUP
SSM_STATE = 64
SSM_LANES = SSM_GROUPS * SSM_STATE
S5_CHUNKS = 4
S5_CW = SSM_LANES // S5_CHUNKS
EPS = 1e-6
ATTN_SCALE = HD ** -0.5
NT_DIMS = (((1,), (1,)), ((), ()))
MIB = 2 ** 20


def _dot(a, b):
    return jnp.dot(a, b, preferred_element_type=F32)


def _rms(x, g):
    ms = jnp.mean(x * x, axis=-1, keepdims=True)
    return x * lax.rsqrt(ms + EPS) * g


def _softmax(s):
    m = jnp.max(s, axis=-1, keepdims=True)
    e = jnp.exp(s - m)
    return e / jnp.sum(e, axis=-1, keepdims=True)


def _const_spec(shape):
    nd = len(shape)
    return pl.BlockSpec(shape, lambda *_: (0,) * nd, pipeline_mode=pl.Buffered(1))


def _layer_spec(shape, layer):
    nd = len(shape)
    return pl.BlockSpec((None,) + shape, lambda *_: (layer,) + (0,) * nd, pipeline_mode=pl.Buffered(1))


def _params(sem, vmem_mib):
    return pltpu.CompilerParams(dimension_semantics=sem, vmem_limit_bytes=vmem_mib * MIB)


def _kv_rows_view(kv5d):
    lead = kv5d.shape[:-3]
    v = kv5d.reshape(lead + (N_MEM, HEADS, HD_TILES, LANES))
    v = jnp.swapaxes(v, -3, -2)
    return v.reshape(lead + (KV_ROWS, LANES))


def _kv_rows_unview(rows4d):
    lead = rows4d.shape[:-2]
    v = rows4d.reshape(lead + (N_MEM, HD_TILES, HEADS, LANES))
    v = jnp.swapaxes(v, -3, -2)
    return v.reshape(lead + (N_MEM, HEADS, HD))


def _head_rows(half, h, n):
    return pl.ds(half * HEADS + h, n, stride=HEADS * HD_TILES)


def _load_head(ref, lead, h):
    parts = [ref[lead + (_head_rows(e, h, N_MEM), slice(None))] for e in range(HD_TILES)]
    return jnp.concatenate(parts, axis=1).astype(BF16)


def _kv_kernel(mem_ref, wk_ref, wv_ref, k_ref, v_ref, wk_sc, wv_sc, *, tm):
    @pl.when(pl.program_id(1) == 0)
    def _():
        wk_sc[...] = wk_ref[...].astype(BF16)
        wv_sc[...] = wv_ref[...].astype(BF16)

    m = mem_ref[...].astype(BF16)
    for w_sc, o_ref in ((wk_sc, k_ref), (wv_sc, v_ref)):
        y = _dot(m, w_sc[...])
        for h in range(HEADS):
            for e in range(HD_TILES):
                c0 = h * HD + e * LANES
                o_ref[0, _head_rows(e, h, tm), :] = y[:, c0:c0 + LANES]


def _mem_kv(mem2d, wk, wv):
    rows = mem2d.shape[0]
    tm = 512
    rpm = HEADS * HD_TILES
    out = jax.ShapeDtypeStruct((DEPTH, rows * rpm, LANES), F32)
    return pl.pallas_call(
        functools.partial(_kv_kernel, tm=tm),
        grid=(DEPTH, rows // tm),
        in_specs=[pl.BlockSpec((tm, D), lambda l, r: (r, 0)),
                  pl.BlockSpec((None, D, D), lambda l, r: (l, 0, 0)),
                  pl.BlockSpec((None, D, D), lambda l, r: (l, 0, 0))],
        out_specs=[pl.BlockSpec((1, tm * rpm, LANES), lambda l, r: (l, r, 0)),
                   pl.BlockSpec((1, tm * rpm, LANES), lambda l, r: (l, r, 0))],
        out_shape=[out, out],
        scratch_shapes=[pltpu.VMEM((D, D), BF16), pltpu.VMEM((D, D), BF16)],
        compiler_params=_params(("arbitrary", "arbitrary"), 40),
        name="mem_kv",
    )(mem2d, wk, wv)


MLP_CH = 1024


def _mlp_kernel(x_ref, g_ref, wup_ref, wdn_ref, gfin_ref, *rest, layer, n_inner, final_norm, emit_bf16):
    if emit_bf16:
        o_ref, wup_b, wdn_b, xn_sc, acc_sc = rest
        wup_b[...] = wup_ref[...].astype(BF16)
        wdn_b[...] = wdn_ref[...].astype(BF16)
    else:
        o_ref, xn_sc, acc_sc = rest
        wup_b, wdn_b = wup_ref, wdn_ref
    c = pl.program_id(1)

    @pl.when(c == 0)
    def _():
        x = x_ref[...]
        xn_sc[...] = _rms(x, g_ref[layer:layer + 1, :]).astype(BF16)
        acc_sc[...] = x

    xn = xn_sc[...]
    acc = acc_sc[...]
    for j in range(n_inner):
        h = _dot(xn, wup_b[:, j * MLP_CH:(j + 1) * MLP_CH])
        h = jnp.maximum(h, 0.0)
        acc = acc + _dot((h * h).astype(BF16), wdn_b[j * MLP_CH:(j + 1) * MLP_CH, :])
    acc_sc[...] = acc

    @pl.when(c == pl.num_programs(1) - 1)
    def _():
        if final_norm:
            o_ref[...] = _rms(acc, gfin_ref[...])
        else:
            o_ref[...] = acc


def _mlp_call(x2d, g, wup, wdn, gfin, layer, w_specs, extra_out_specs, extra_out_shapes, *, tm, hid_steps,
              final_norm):
    rows = x2d.shape[0]
    kern = functools.partial(_mlp_kernel, layer=layer, n_inner=D_FF // hid_steps // MLP_CH, final_norm=final_norm,
                             emit_bf16=bool(extra_out_specs))
    return pl.pallas_call(
        kern,
        grid=(rows // tm, hid_steps),
        in_specs=[pl.BlockSpec((tm, D), lambda r, c: (r, 0)), _const_spec((DEPTH, D))] + w_specs + [
                  _const_spec((1, D))],
        out_specs=[pl.BlockSpec((tm, D), lambda r, c: (r, 0))] + extra_out_specs,
        out_shape=[jax.ShapeDtypeStruct((rows, D), F32)] + extra_out_shapes,
        scratch_shapes=[pltpu.VMEM((tm, D), BF16), pltpu.VMEM((tm, D), F32)],
        compiler_params=_params(("arbitrary", "arbitrary"), 48),
        name="mlp",
    )(x2d, g, wup, wdn, gfin)


def _mlp_resident(x2d, g, wup_b, wdn_b, gfin, layer, *, tm, final_norm):
    w_specs = [_const_spec((D, D_FF)), _const_spec((D_FF, D))]
    return _mlp_call(x2d, g, wup_b, wdn_b, gfin, layer, w_specs, [], [], tm=tm, hid_steps=1,
                     final_norm=final_norm)[0]


def _mlp_streaming(x2d, g, wup, wdn, gfin, layer, *, tm, hid_steps, final_norm):
    hid_blk = D_FF // hid_steps
    w_specs = [pl.BlockSpec((None, D, hid_blk), lambda r, c: (layer, 0, c)),
               pl.BlockSpec((None, hid_blk, D), lambda r, c: (layer, c, 0))]
    b_specs = [pl.BlockSpec((D, hid_blk), lambda r, c: (0, c)), pl.BlockSpec((hid_blk, D), lambda r, c: (c, 0))]
    b_shapes = [jax.ShapeDtypeStruct((D, D_FF), BF16), jax.ShapeDtypeStruct((D_FF, D), BF16)]
    return _mlp_call(x2d, g, wup, wdn, gfin, layer, w_specs, b_specs, b_shapes, tm=tm, hid_steps=hid_steps,
                     final_norm=final_norm)


def _attn_prompt_kernel(x_ref, g_ref, wq_ref, k_ref, v_ref, wo_ref, o_ref, *, layer, tq, sub):
    starts = range(0, tq, sub)
    xs = [x_ref[0, r0:r0 + sub, :] for r0 in starts]
    qs = [(_dot(_rms(x, g_ref[layer:layer + 1, :]).astype(BF16), wq_ref[...]) * ATTN_SCALE).astype(BF16) for x in xs]
    outs = [[] for _ in starts]
    for h in range(HEADS):
        cols = slice(h * HD, (h + 1) * HD)
        kh = _load_head(k_ref, (), h)
        vh = _load_head(v_ref, (), h)
        for q, out in zip(qs, outs):
            s = lax.dot_general(q[:, cols], kh, NT_DIMS, preferred_element_type=F32)
            out.append(_dot(_softmax(s).astype(BF16), vh).astype(BF16))
    for r0, x, out in zip(starts, xs, outs):
        o_ref[0, r0:r0 + sub, :] = x + _dot(jnp.concatenate(out, axis=-1), wo_ref[...])


def _attn_prompt(x3d, g, wq, k4d, v4d, wo, layer, *, tq):
    b, l, _ = x3d.shape
    kv_spec = pl.BlockSpec((None, None, KV_ROWS, LANES), lambda i, t: (layer, i, 0, 0))
    return pl.pallas_call(
        functools.partial(_attn_prompt_kernel, layer=layer, tq=tq, sub=ATTN_SUB),
        grid=(b, l // tq),
        in_specs=[pl.BlockSpec((1, tq, D), lambda i, t: (i, t, 0)),
                  _const_spec((DEPTH, D)),
                  _const_spec((D, D)),
                  kv_spec, kv_spec,
                  _const_spec((D, D))],
        out_specs=pl.BlockSpec((1, tq, D), lambda i, t: (i, t, 0)),
        out_shape=jax.ShapeDtypeStruct(x3d.shape, F32),
        compiler_params=_params(("arbitrary", "arbitrary"), 56),
        name="attn_prompt",
    )(x3d, g, wq, k4d, v4d, wo)


def _attn_sample_kernel(x_ref, g_ref, wq_ref, k_hbm, v_hbm, wo_ref, o_ref, wq_b, wo_b, q_sc, o_sc, kbuf, vbuf, sem,
                        *, layer, sb, seq_len):
    i = pl.program_id(0)
    n_steps = pl.num_programs(0)

    def kv_copies(step, slot):
        seqs = pl.ds(step * sb, sb)
        return (pltpu.make_async_copy(k_hbm.at[layer, seqs], kbuf.at[slot], sem.at[0, slot]),
                pltpu.make_async_copy(v_hbm.at[layer, seqs], vbuf.at[slot], sem.at[1, slot]))

    @pl.when(i == 0)
    def _():
        for step in range(KV_RING - 1):
            for copy in kv_copies(step, step):
                copy.start()

    @pl.when(i + KV_RING - 1 < n_steps)
    def _():
        for copy in kv_copies(i + KV_RING - 1, (i + KV_RING - 1) % KV_RING):
            copy.start()

    @pl.when(i == 0)
    def _():
        wq_b[...] = wq_ref[...].astype(BF16)
        wo_b[...] = wo_ref[...].astype(BF16)
        xn = _rms(x_ref[...], g_ref[layer:layer + 1, :]).astype(BF16)
        q_sc[...] = _dot(xn, wq_b[...]) * ATTN_SCALE

    pair_rows = 2 * seq_len
    first_seq_row = lax.broadcasted_iota(jnp.int32, (pair_rows, HD), 0) < seq_len

    slot = i % KV_RING
    for copy in kv_copies(i, slot):
        copy.wait()
    k_ref = kbuf.at[slot]
    v_ref = vbuf.at[slot]

    for j in range(sb // 2):
        r0 = pl.multiple_of((i * sb + 2 * j) * seq_len, pair_rows)
        q8 = q_sc[pl.ds(r0, pair_rows), :].astype(BF16)
        scores = [lax.dot_general(q8[:, h * HD:(h + 1) * HD], _load_head(k_ref, (2 * j + s,), h), NT_DIMS,
                                  preferred_element_type=F32)
                  for s in range(2) for h in range(HEADS)]
        p = _softmax(jnp.concatenate(scores, axis=0))
        res = []
        for h in range(HEADS):
            oh = []
            for s in range(2):
                blk = (s * HEADS + h) * pair_rows
                oh.append(_dot(p[blk:blk + pair_rows, :].astype(BF16), _load_head(v_ref, (2 * j + s,), h)))
            res.append(jnp.where(first_seq_row, oh[0], oh[1]))
        o_sc[pl.ds(r0, pair_rows), :] = jnp.concatenate(res, axis=-1)

    @pl.when(i == pl.num_programs(0) - 1)
    def _():
        o_ref[...] = x_ref[...] + _dot(o_sc[...].astype(BF16), wo_b[...])


def _attn_sample(x2d, g, wq, k4d, v4d, wo, layer, *, sb, seq_len):
    rows = x2d.shape[0]
    nseq = rows // seq_len
    assert 2 * seq_len == 8 and nseq % sb == 0 and sb % 2 == 0
    assert nseq // sb >= KV_RING
    kern = functools.partial(_attn_sample_kernel, layer=layer, sb=sb, seq_len=seq_len)
    kv_spec = pl.BlockSpec(memory_space=pl.ANY)
    return pl.pallas_call(
        kern,
        grid=(nseq // sb,),
        in_specs=[_const_spec((rows, D)),
                  _const_spec((DEPTH, D)),
                  _layer_spec((D, D), layer),
                  kv_spec, kv_spec,
                  _layer_spec((D, D), layer)],
        out_specs=[pl.BlockSpec((rows, D), lambda i: (0, 0)),
                   pl.BlockSpec((D, D), lambda i: (0, 0)),
                   pl.BlockSpec((D, D), lambda i: (0, 0))],
        out_shape=[jax.ShapeDtypeStruct((rows, D), F32),
                   jax.ShapeDtypeStruct((D, D), BF16),
                   jax.ShapeDtypeStruct((D, D), BF16)],
        scratch_shapes=[pltpu.VMEM((rows, D), F32), pltpu.VMEM((rows, D), F32),
                        pltpu.VMEM((KV_RING, sb, KV_ROWS, LANES), F32),
                        pltpu.VMEM((KV_RING, sb, KV_ROWS, LANES), F32),
                        pltpu.SemaphoreType.DMA((2, KV_RING))],
        compiler_params=_params(("arbitrary",), 58),
        name="attn_sample",
    )(x2d, g, wq, k4d, v4d, wo)


def _glu_in(x, g, win_ref, bin_ref):
    xn = _rms(x, g).astype(BF16)
    h = _dot(xn, win_ref[...]) + bin_ref[...]
    return h[:, :D] * jax.nn.sigmoid(h[:, D:])


def _ln_silu_out(c, lng_ref, lnb_ref, wout_ref):
    mu = jnp.mean(c, axis=-1, keepdims=True)
    cc = c - mu
    var = jnp.mean(cc * cc, axis=-1, keepdims=True)
    n = cc * lax.rsqrt(var + EPS) * lng_ref[...] + lnb_ref[...]
    s = (n * jax.nn.sigmoid(n)).astype(BF16)
    return _dot(s, wout_ref[...])


def _conv_prompt_kernel(x_ref, g_ref, win_ref, bin_ref, dw_ref, dwb_ref, lng_ref, lnb_ref, wout_ref,
                        o_ref, hist_ref, gbuf, cbuf, *, tl, ns):
    n_tiles = D // LANES
    n_groups = tl // 8
    base = HIST_PAD - CONV_BUF

    def rows2(start, n):
        return pl.ds(2 * start, n, stride=2)

    @pl.when(pl.program_id(1) == 0)
    def _():
        for s in range(ns):
            for j in range(n_tiles):
                gbuf[s, j, rows2(0, HIST_PAD), :] = jnp.zeros((HIST_PAD, LANES), F32)

    xs = [x_ref[s] for s in range(ns)]
    xns = [_rms(x, g_ref[0:1, :]).astype(BF16) for x in xs]

    def glu_block(s, c):
        va = slice(c * 256, (c + 1) * 256)
        ga = slice(D + c * 256, D + (c + 1) * 256)
        val = _dot(xns[s], win_ref[:, va]) + bin_ref[:, va]
        gate = _dot(xns[s], win_ref[:, ga]) + bin_ref[:, ga]
        gl = val * jax.nn.sigmoid(gate)
        for jj in range(256 // LANES):
            gbuf[s, 2 * c + jj, rows2(HIST_PAD, tl), :] = gl[:, jj * LANES:(jj + 1) * LANES]

    def conv_tile(s, j):
        lanes = slice(j * LANES, (j + 1) * LANES)
        w = [jnp.broadcast_to(dw_ref[k:k + 1, lanes], (8, LANES)) for k in range(CONV_W)]
        bias = jnp.broadcast_to(dwb_ref[:, lanes], (8, LANES))
        accs = [bias] * 4
        for i in range(n_groups + 3):
            lags = range(max(0, i - n_groups + 1), min(i, 3) + 1)
            for r in range(8):
                taps = [(a, 8 * a + r) for a in lags if 8 * a + r < CONV_W]
                if taps:
                    wnd = gbuf[s, j, rows2(base + 8 * i + r, 8), :]
                    for a, k in taps:
                        accs[a] = accs[a] + w[k] * wnd
            if i >= 3:
                cbuf[s, (i - 3) * 8:(i - 2) * 8, lanes] = accs[3]
            accs = [bias, accs[0], accs[1], accs[2]]

    def finish(s):
        o_ref[s] = xs[s] + _ln_silu_out(cbuf[s], lng_ref, lnb_ref, wout_ref)
        for j in range(n_tiles):
            tail = gbuf[s, j, rows2(tl, HIST_PAD), :]
            hist_ref[s, :, j * LANES:(j + 1) * LANES] = tail
            gbuf[s, j, rows2(0, HIST_PAD), :] = tail

    for c in range(D // 256):
        glu_block(0, c)
    for s in range(ns):
        for j in range(n_tiles):
            conv_tile(s, j)
            if s + 1 < ns and j % 2 == 1:
                glu_block(s + 1, j // 2)
        finish(s)


def _conv_param_specs():
    return [_const_spec((DEPTH, D)),
            _layer_spec((D, 2 * D), 0),
            _const_spec((1, 2 * D)),
            _layer_spec((CONV_W, D), 0),
            _const_spec((1, D)),
            _const_spec((1, D)),
            _const_spec((1, D)),
            _layer_spec((D, D), 0)]


def _conv_prompt(x3d, conv_p, *, tl):
    b, l, _ = x3d.shape
    ns = CONV_NS
    assert b % ns == 0
    kern = functools.partial(_conv_prompt_kernel, tl=tl, ns=ns)
    return pl.pallas_call(
        kern,
        grid=(b // ns, l // tl),
        in_specs=[pl.BlockSpec((ns, tl, D), lambda i, t: (i, t, 0))] + _conv_param_specs(),
        out_specs=[pl.BlockSpec((ns, tl, D), lambda i, t: (i, t, 0)),
                   pl.BlockSpec((ns, HIST_PAD, D), lambda i, t: (i, 0, 0))],
        out_shape=[jax.ShapeDtypeStruct(x3d.shape, F32),
                   jax.ShapeDtypeStruct((b, HIST_PAD, D), F32)],
        scratch_shapes=[pltpu.VMEM((ns, D // LANES, 2 * (HIST_PAD + tl), LANES), F32),
                        pltpu.VMEM((ns, tl, D), F32)],
        compiler_params=_params(("arbitrary", "arbitrary"), 40),
        name="conv_prompt",
    )(x3d, *conv_p)


def _conv_sample_kernel(x_ref, hist_ref, g_ref, win_ref, bin_ref, dw_ref, dwb_ref, lng_ref, lnb_ref, wout_ref,
                        o_ref, nhist_ref, *, nb, seq_len):
    xt = jnp.concatenate([x_ref[:, t * D:(t + 1) * D] for t in range(seq_len)], axis=0)
    gl = _glu_in(xt, g_ref[0:1, :], win_ref, bin_ref)
    g_t = [gl[t * nb:(t + 1) * nb, :] for t in range(seq_len)]

    def padded(r, lanes):
        if r < CONV_BUF:
            return hist_ref[r, :, lanes]
        return g_t[r - CONV_BUF][:, lanes]

    cs = []
    for t in range(seq_len):
        tiles = []
        for j in range(D // LANES):
            lanes = slice(j * LANES, (j + 1) * LANES)
            acc = None
            for k in range(CONV_W):
                term = dw_ref[k:k + 1, lanes] * padded(t + k, lanes)
                acc = term if acc is None else acc + term
            tiles.append(acc)
        cs.append(jnp.concatenate(tiles, axis=-1))
    c = jnp.concatenate(cs, axis=0) + dwb_ref[...]
    res = _ln_silu_out(c, lng_ref, lnb_ref, wout_ref)
    for t in range(seq_len):
        o_ref[:, t * D:(t + 1) * D] = x_ref[:, t * D:(t + 1) * D] + res[t * nb:(t + 1) * nb, :]
    keep = CONV_BUF - seq_len
    nhist_ref[0:keep] = hist_ref[seq_len:CONV_BUF]
    for t in range(seq_len):
        nhist_ref[keep + t] = g_t[t]


def _conv_sample(xv, hist3, conv_p, *, nb, seq_len):
    nseq = xv.shape[0]
    assert seq_len <= CONV_BUF and nseq % nb == 0 and nb % 8 == 0
    kern = functools.partial(_conv_sample_kernel, nb=nb, seq_len=seq_len)
    hist_spec = pl.BlockSpec((CONV_BUF, nb, D), lambda i: (0, i, 0))
    return pl.pallas_call(
        kern,
        grid=(nseq // nb,),
        in_specs=[pl.BlockSpec((nb, seq_len * D), lambda i: (i, 0)), hist_spec] + _conv_param_specs(),
        out_specs=[pl.BlockSpec((nb, seq_len * D), lambda i: (i, 0)), hist_spec],
        out_shape=[jax.ShapeDtypeStruct(xv.shape, F32),
                   jax.ShapeDtypeStruct(hist3.shape, F32)],
        compiler_params=_params(("arbitrary",), 40),
        name="conv_sample",
    )(xv, hist3, *conv_p)


def _s5_scan_chunk_static(hs, st, are_ref, aim_ref, k, tl):
    ar = are_ref[:, k * S5_CW:(k + 1) * S5_CW]
    ai = aim_ref[:, k * S5_CW:(k + 1) * S5_CW]
    cre = slice(2 * k * S5_CW, (2 * k + 1) * S5_CW)
    cim = slice((2 * k + 1) * S5_CW, (2 * k + 2) * S5_CW)
    hr, hi = st[:, cre], st[:, cim]
    for t in range(tl):
        rows = slice(t * 8, (t + 1) * 8)
        hr, hi = ar * hr - ai * hi + hs[rows, cre], ar * hi + ai * hr + hs[rows, cim]
        hs[rows, cre] = hr
        hs[rows, cim] = hi
    st[:, cre] = hr
    st[:, cim] = hi


def _s5_scan(hs, st, are_ref, aim_ref, nb, tl):
    for k in range(S5_CHUNKS):
        ar = are_ref[:, k * S5_CW:(k + 1) * S5_CW]
        ai = aim_ref[:, k * S5_CW:(k + 1) * S5_CW]
        cre = slice(2 * k * S5_CW, (2 * k + 1) * S5_CW)
        cim = slice((2 * k + 1) * S5_CW, (2 * k + 2) * S5_CW)

        def seq_group(bg, carry, ar=ar, ai=ai, cre=cre, cim=cim):
            b0 = pl.multiple_of(bg * 8, 8)

            def step(t, h):
                hr, hi = h
                row = pl.multiple_of(t * nb + b0, 8)
                nr = ar * hr - ai * hi + hs[pl.ds(row, 8), cre]
                ni = ar * hi + ai * hr + hs[pl.ds(row, 8), cim]
                hs[pl.ds(row, 8), cre] = nr
                hs[pl.ds(row, 8), cim] = ni
                return nr, ni

            hr, hi = lax.fori_loop(0, tl, step, (st[pl.ds(b0, 8), cre], st[pl.ds(b0, 8), cim]),
                                   unroll=min(tl, 4))
            st[pl.ds(b0, 8), cre] = hr
            st[pl.ds(b0, 8), cim] = hi
            return carry

        lax.fori_loop(0, nb // 8, seq_group, 0)


def _s5_core(ut, ut_b, bblk_ref, are_ref, aim_ref, cblk_ref, d_ref, hs, st, nb, tl):
    for k in range(S5_CHUNKS):
        hs[:, 2 * k * S5_CW:(2 * k + 2) * S5_CW] = _dot(ut_b[:, k * 256:(k + 1) * 256], bblk_ref[k])
    if nb == 8:
        for k in range(S5_CHUNKS):
            _s5_scan_chunk_static(hs, st, are_ref, aim_ref, k, tl)
    else:
        _s5_scan(hs, st, are_ref, aim_ref, nb, tl)
    ys = [lax.dot_general(hs[:, 2 * k * S5_CW:(2 * k + 2) * S5_CW].astype(BF16), cblk_ref[k], NT_DIMS,
                          preferred_element_type=F32) for k in range(S5_CHUNKS)]
    y = jnp.concatenate(ys, axis=-1) + d_ref[...] * ut
    return jax.nn.gelu(y, approximate=True).astype(BF16)


def _s5_state_in(st, h0re_ref, h0im_ref):
    for k in range(S5_CHUNKS):
        st[:, 2 * k * S5_CW:(2 * k + 1) * S5_CW] = h0re_ref[:, k * S5_CW:(k + 1) * S5_CW]
        st[:, (2 * k + 1) * S5_CW:(2 * k + 2) * S5_CW] = h0im_ref[:, k * S5_CW:(k + 1) * S5_CW]


def _s5_state_out(st, hre_ref, him_ref):
    for k in range(S5_CHUNKS):
        hre_ref[:, k * S5_CW:(k + 1) * S5_CW] = st[:, 2 * k * S5_CW:(2 * k + 1) * S5_CW]
        him_ref[:, k * S5_CW:(k + 1) * S5_CW] = st[:, (2 * k + 1) * S5_CW:(2 * k + 2) * S5_CW]


def _glu_out(z):
    return z[:, :D] * jax.nn.sigmoid(z[:, D:])


def _s5_param_specs():
    return [_const_spec((DEPTH, D)),
            _const_spec((S5_CHUNKS, 256, 2 * S5_CW)),
            _const_spec((8, SSM_LANES)),
            _const_spec((8, SSM_LANES)),
            _const_spec((S5_CHUNKS, 256, 2 * S5_CW)),
            _const_spec((1, D)),
            _layer_spec((D, 2 * D), 0)]


def _s5_prompt_kernel(x_ref, g_ref, bblk_ref, are_ref, aim_ref, cblk_ref, d_ref, wglu_ref,
                      h0re_ref, h0im_ref, o_ref, hre_ref, him_ref, hs, st, ubuf, gtmp, *, nb, tl, n_sub):
    t = pl.program_id(0)
    rows = nb * tl
    n_tiles = D // LANES
    pitch = tl + 8

    @pl.when(t == 0)
    def _():
        _s5_state_in(st, h0re_ref, h0im_ref)

    def chunk_cols(k):
        return slice(2 * k * S5_CW, (2 * k + 2) * S5_CW)

    def c_proj(i, k):
        return lax.dot_general(hs[i, :, chunk_cols(k)].astype(BF16), cblk_ref[k], NT_DIMS,
                               preferred_element_type=F32)

    xs, uts = [], []
    for i in range(n_sub):
        x = x_ref[:, i * tl:(i + 1) * tl, :].reshape(rows, D)
        u = _rms(x, g_ref[1:2, :])
        for j in range(n_tiles):
            for b in range(nb):
                ubuf[i, j, b * pitch:b * pitch + tl, :] = u[b * tl:(b + 1) * tl, j * LANES:(j + 1) * LANES]
        ut = jnp.concatenate(
            [jnp.concatenate([ubuf[i, j, pl.ds(step, nb, stride=pitch), :] for j in range(n_tiles)], axis=1)
             for step in range(tl)], axis=0)
        ut_b = ut.astype(BF16)
        for k in range(S5_CHUNKS):
            hs[i, :, chunk_cols(k)] = _dot(ut_b[:, k * 256:(k + 1) * 256], bblk_ref[k])
        xs.append(x)
        uts.append(ut)

    ys = [[None] * S5_CHUNKS for _ in range(n_sub)]
    for i in range(n_sub):
        for k in range(S5_CHUNKS):
            _s5_scan_chunk_static(hs.at[i], st, are_ref, aim_ref, k, tl)
            if i > 0:
                ys[i - 1][k] = c_proj(i - 1, k)
    for k in range(S5_CHUNKS):
        ys[n_sub - 1][k] = c_proj(n_sub - 1, k)

    for i in range(n_sub):
        y = jnp.concatenate(ys[i], axis=-1) + d_ref[...] * uts[i]
        ge = jax.nn.gelu(y, approximate=True)
        for j in range(n_tiles):
            gtmp[i, j] = ge[:, j * LANES:(j + 1) * LANES]
        ge = jnp.concatenate(
            [jnp.concatenate([gtmp[i, j, pl.ds(b, tl, stride=nb), :] for j in range(n_tiles)], axis=1)
             for b in range(nb)], axis=0).astype(BF16)
        out = _glu_out(_dot(ge, wglu_ref[...]))
        o_ref[:, i * tl:(i + 1) * tl, :] = (xs[i] + out).reshape(nb, tl, D)

    @pl.when(t == pl.num_programs(0) - 1)
    def _():
        _s5_state_out(st, hre_ref, him_ref)


def _s5_prompt(x3d, s5_p, h0re, h0im, *, tl):
    nb, l, _ = x3d.shape
    assert nb % 8 == 0 and nb & (nb - 1) == 0 and tl & (tl - 1) == 0
    rows = nb * tl
    n_sub = S5_SUB
    step = n_sub * tl
    kern = functools.partial(_s5_prompt_kernel, nb=nb, tl=tl, n_sub=n_sub)
    st_shape = jax.ShapeDtypeStruct((nb, SSM_LANES), F32)
    return pl.pallas_call(
        kern,
        grid=(l // step,),
        in_specs=[pl.BlockSpec((nb, step, D), lambda t: (0, t, 0))] + _s5_param_specs() + [
                  _const_spec((nb, SSM_LANES)),
                  _const_spec((nb, SSM_LANES))],
        out_specs=[pl.BlockSpec((nb, step, D), lambda t: (0, t, 0)),
                   pl.BlockSpec((nb, SSM_LANES), lambda t: (0, 0)),
                   pl.BlockSpec((nb, SSM_LANES), lambda t: (0, 0))],
        out_shape=[jax.ShapeDtypeStruct(x3d.shape, F32), st_shape, st_shape],
        scratch_shapes=[pltpu.VMEM((n_sub, rows, 2 * SSM_LANES), F32), pltpu.VMEM((nb, 2 * SSM_LANES), F32),
                        pltpu.VMEM((n_sub, D // LANES, nb * (tl + 8), LANES), F32),
                        pltpu.VMEM((n_sub, D // LANES, rows, LANES), F32)],
        compiler_params=_params(("arbitrary",), 56),
        name="s5_prompt",
    )(x3d, *s5_p, h0re, h0im)


def _s5_sample_kernel(x_ref, g_ref, bblk_ref, are_ref, aim_ref, cblk_ref, d_ref, wglu_ref,
                      h0re_ref, h0im_ref, o_ref, hre_ref, him_ref, hs, st, *, nb, seq_len):
    _s5_state_in(st, h0re_ref, h0im_ref)
    xt = jnp.concatenate([x_ref[:, t * D:(t + 1) * D] for t in range(seq_len)], axis=0)
    ut = _rms(xt, g_ref[1:2, :])
    ge = _s5_core(ut, ut.astype(BF16), bblk_ref, are_ref, aim_ref, cblk_ref, d_ref, hs, st, nb, seq_len)
    out = _glu_out(_dot(ge, wglu_ref[...]))
    for t in range(seq_len):
        o_ref[:, t * D:(t + 1) * D] = x_ref[:, t * D:(t + 1) * D] + out[t * nb:(t + 1) * nb, :]
    _s5_state_out(st, hre_ref, him_ref)


def _s5_sample(xv, s5_p, h0re, h0im, *, nb, seq_len):
    nseq = xv.shape[0]
    assert nseq % nb == 0 and nb % 8 == 0
    rows = nb * seq_len
    kern = functools.partial(_s5_sample_kernel, nb=nb, seq_len=seq_len)
    st_shape = jax.ShapeDtypeStruct((nseq, SSM_LANES), F32)
    st_spec = pl.BlockSpec((nb, SSM_LANES), lambda i: (i, 0))
    return pl.pallas_call(
        kern,
        grid=(nseq // nb,),
        in_specs=[pl.BlockSpec((nb, seq_len * D), lambda i: (i, 0))] + _s5_param_specs() + [st_spec, st_spec],
        out_specs=[pl.BlockSpec((nb, seq_len * D), lambda i: (i, 0)), st_spec, st_spec],
        out_shape=[jax.ShapeDtypeStruct(xv.shape, F32), st_shape, st_shape],
        scratch_shapes=[pltpu.VMEM((rows, 2 * SSM_LANES), F32), pltpu.VMEM((nb, 2 * SSM_LANES), F32)],
        compiler_params=_params(("arbitrary",), 48),
        name="s5_sample",
    )(xv, *s5_p, h0re, h0im)


def _s5_discretize(a_re, a_im, log_dt, b_re, b_im, c_re, c_im):
    dt = jnp.exp(log_dt)[:, None]
    mag = jnp.exp(dt * a_re)
    ab_re = mag * jnp.cos(dt * a_im)
    ab_im = mag * jnp.sin(dt * a_im)
    den = a_re * a_re + a_im * a_im
    num_re = ab_re - 1.0
    coef_re = (num_re * a_re + ab_im * a_im) / den
    coef_im = (ab_im * a_re - num_re * a_im) / den
    bb_re = coef_re[..., None] * b_re - coef_im[..., None] * b_im
    bb_im = coef_re[..., None] * b_im + coef_im[..., None] * b_re
    gpc = SSM_GROUPS // S5_CHUNKS
    cols = 2 * S5_CW
    col_group = (lax.broadcasted_iota(jnp.int32, (gpc, 1, cols), 2) // SSM_STATE) % gpc
    own = col_group == lax.broadcasted_iota(jnp.int32, (gpc, 1, cols), 0)

    def block_diag(compact):
        full = jnp.where(own[None], compact[:, None], 0.0).astype(BF16)
        return full.reshape(S5_CHUNKS, gpc * SSM_GROUP, cols)

    bb = jnp.stack([bb_re, bb_im]).reshape(2, S5_CHUNKS, gpc, SSM_STATE, SSM_GROUP)
    bblk = block_diag(jnp.transpose(bb, (1, 4, 0, 2, 3)).reshape(S5_CHUNKS, SSM_GROUP, cols))
    cc = jnp.stack([c_re, -c_im]).reshape(2, S5_CHUNKS, gpc, SSM_GROUP, SSM_STATE)
    cblk = block_diag(jnp.transpose(cc, (1, 3, 0, 2, 4)).reshape(S5_CHUNKS, SSM_GROUP, cols))
    are = jnp.broadcast_to(ab_re.reshape(1, SSM_LANES), (8, SSM_LANES))
    aim = jnp.broadcast_to(ab_im.reshape(1, SSM_LANES), (8, SSM_LANES))
    return bblk, cblk, are, aim


S5_TL = 32
S5_SUB = 2
CONV_TL = 512
CONV_NS = 1
KV_RING = 3
ATTN_TQ = 2048
ATTN_SUB = 256
MLP_TM = 1024


def kernel(x_prompt, x_sample, mem_prompt, cache_conv, state_s5_re, state_s5_im, cache_mem_k, cache_mem_v,
           norm_mix, norm_xattn, norm_ffn, norm_final,
           conv_w_in, conv_b_in, conv_dw, conv_dw_b, conv_ln_g, conv_ln_b, conv_w_out,
           s5_a_re, s5_a_im, s5_log_dt, s5_b_re, s5_b_im, s5_c_re, s5_c_im, s5_d, s5_w_glu,
           xattn_w_q, xattn_w_k, xattn_w_v, xattn_w_o, mlp_w_up, mlp_w_down):
    bp, lp, _ = x_prompt.shape
    bs, ls, _ = x_sample.shape

    conv_p = (norm_mix, conv_w_in.astype(BF16), conv_b_in, conv_dw, conv_dw_b, conv_ln_g, conv_ln_b,
              conv_w_out.astype(BF16))
    bblk, cblk, are, aim = _s5_discretize(s5_a_re[0], s5_a_im[0], s5_log_dt[0], s5_b_re[0], s5_b_im[0],
                                          s5_c_re[0], s5_c_im[0])
    s5_p = (norm_mix, bblk, are, aim, cblk, s5_d, s5_w_glu.astype(BF16))
    gfin = norm_final.reshape(1, D)

    kp, vp = _mem_kv(mem_prompt.reshape(bp * N_MEM, D), xattn_w_k, xattn_w_v)
    kp4 = kp.reshape(DEPTH, bp, KV_ROWS, LANES)
    vp4 = vp.reshape(DEPTH, bp, KV_ROWS, LANES)
    mem_k_prompt = _kv_rows_unview(kp4)
    mem_v_prompt = _kv_rows_unview(vp4)

    def attn_sample(x2d, i):
        return _attn_sample(x2d, norm_xattn, xattn_w_q, ck, cv, xattn_w_o, i, sb=4, seq_len=ls)

    def mlp_sample(x2d, i, final):
        return _mlp_streaming(x2d, norm_ffn, mlp_w_up, mlp_w_down, gfin, i, tm=min(MLP_TM, bs * ls), hid_steps=4,
                              final_norm=final)

    xp_conv, hist = _conv_prompt(x_prompt, conv_p, tl=CONV_TL)
    (xp_conv, hist), x_sample = lax.optimization_barrier(((xp_conv, hist), x_sample))

    ck = _kv_rows_view(cache_mem_k)
    cv = _kv_rows_view(cache_mem_v)
    hist3 = jnp.transpose(cache_conv[0], (1, 0, 2))
    xv, nhist = _conv_sample(x_sample.reshape(bs, ls * D), hist3, conv_p, nb=32, seq_len=ls)
    conv_sample = jnp.transpose(nhist, (1, 0, 2))[None]
    x, wq0, wo0 = attn_sample(xv.reshape(bs * ls, D), 0)
    x, wup0, wdn0 = mlp_sample(x, 0, False)
    xv, hre, him = _s5_sample(x.reshape(bs, ls * D), s5_p,
                              state_s5_re[0].reshape(bs, SSM_LANES), state_s5_im[0].reshape(bs, SSM_LANES),
                              nb=64, seq_len=ls)
    s5_re_sample = hre.reshape(1, bs, SSM_GROUPS, SSM_STATE)
    s5_im_sample = him.reshape(1, bs, SSM_GROUPS, SSM_STATE)
    x, wq1, wo1 = attn_sample(xv.reshape(bs * ls, D), 1)
    x, wup1, wdn1 = mlp_sample(x, 1, True)
    y_sample = x.reshape(bs, ls, D)

    def mlp_prompt(x3d, wup_b, wdn_b, i, final):
        y = _mlp_resident(x3d.reshape(bp * lp, D), norm_ffn, wup_b, wdn_b, gfin, i, tm=MLP_TM, final_norm=final)
        return y.reshape(bp, lp, D)

    conv_prompt = hist[None, :, HIST_PAD - CONV_BUF:, :]
    x = _attn_prompt(xp_conv, norm_xattn, wq0, kp4, vp4, wo0, 0, tq=ATTN_TQ)
    x = mlp_prompt(x, wup0, wdn0, 0, False)
    zeros_state = jnp.zeros((bp, SSM_LANES), F32)
    x, hre, him = _s5_prompt(x, s5_p, zeros_state, zeros_state, tl=S5_TL)
    s5_re_prompt = hre.reshape(1, bp, SSM_GROUPS, SSM_STATE)
    s5_im_prompt = him.reshape(1, bp, SSM_GROUPS, SSM_STATE)
    x = _attn_prompt(x, norm_xattn, wq1, kp4, vp4, wo1, 1, tq=ATTN_TQ)
    y_prompt = mlp_prompt(x, wup1, wdn1, 1, True)

    return (y_prompt, y_sample, conv_prompt, conv_sample, s5_re_prompt, s5_im_prompt,
            s5_re_sample, s5_im_sample, mem_k_prompt, mem_v_prompt)
```

```python
import functools

import jax
import jax.numpy as jnp
from jax import lax
from jax.experimental import pallas as pl
from jax.experimental.pallas import tpu as pltpu

F32 = jnp.float32
BF16 = jnp.bfloat16

LANES = 128
D = 1024
DEPTH = 2
D_FF = 4 * D
N_MEM = 256
HEADS = 4
HD = D // HEADS
HD_TILES = HD // LANES
KV_ROWS = N_MEM * HEADS * HD_TILES
CONV_W = 31
CONV_BUF = CONV_W - 1
HIST_PAD = 32
SSM_GROUP = 16
SSM_GROUPS = D // SSM_GROUP
SSM_STATE = 64
SSM_LANES = SSM_GROUPS * SSM_STATE
S5_CHUNKS = 4
S5_CW = SSM_LANES // S5_CHUNKS
EPS = 1e-6
ATTN_SCALE = HD ** -0.5
NT_DIMS = (((1,), (1,)), ((), ()))
MIB = 2 ** 20


def _dot(a, b):
    return jnp.dot(a, b, preferred_element_type=F32)


def _rms(x, g):
    ms = jnp.mean(x * x, axis=-1, keepdims=True)
    return x * lax.rsqrt(ms + EPS) * g


def _softmax(s):
    m = jnp.max(s, axis=-1, keepdims=True)
    e = jnp.exp(s - m)
    return e / jnp.sum(e, axis=-1, keepdims=True)


def _const_spec(shape):
    nd = len(shape)
    return pl.BlockSpec(shape, lambda *_: (0,) * nd, pipeline_mode=pl.Buffered(1))


def _layer_spec(shape, layer):
    nd = len(shape)
    return pl.BlockSpec((None,) + shape, lambda *_: (layer,) + (0,) * nd, pipeline_mode=pl.Buffered(1))


def _params(sem, vmem_mib):
    return pltpu.CompilerParams(dimension_semantics=sem, vmem_limit_bytes=vmem_mib * MIB)


def _kv_rows_view(kv5d):
    lead = kv5d.shape[:-3]
    v = kv5d.reshape(lead + (N_MEM, HEADS, HD_TILES, LANES))
    v = jnp.swapaxes(v, -3, -2)
    return v.reshape(lead + (KV_ROWS, LANES))


def _kv_rows_unview(rows4d):
    lead = rows4d.shape[:-2]
    v = rows4d.reshape(lead + (N_MEM, HD_TILES, HEADS, LANES))
    v = jnp.swapaxes(v, -3, -2)
    return v.reshape(lead + (N_MEM, HEADS, HD))


def _head_rows(half, h, n):
    return pl.ds(half * HEADS + h, n, stride=HEADS * HD_TILES)


def _load_head(ref, lead, h):
    parts = [ref[lead + (_head_rows(e, h, N_MEM), slice(None))] for e in range(HD_TILES)]
    return jnp.concatenate(parts, axis=1).astype(BF16)


def _kv_kernel(mem_ref, wk_ref, wv_ref, k_ref, v_ref, wk_sc, wv_sc, *, tm):
    @pl.when(pl.program_id(1) == 0)
    def _():
        wk_sc[...] = wk_ref[...].astype(BF16)
        wv_sc[...] = wv_ref[...].astype(BF16)

    m = mem_ref[...].astype(BF16)
    for w_sc, o_ref in ((wk_sc, k_ref), (wv_sc, v_ref)):
        y = _dot(m, w_sc[...])
        for h in range(HEADS):
            for e in range(HD_TILES):
                c0 = h * HD + e * LANES
                o_ref[0, _head_rows(e, h, tm), :] = y[:, c0:c0 + LANES]


def _mem_kv(mem2d, wk, wv):
    rows = mem2d.shape[0]
    tm = 512
    rpm = HEADS * HD_TILES
    out = jax.ShapeDtypeStruct((DEPTH, rows * rpm, LANES), F32)
    return pl.pallas_call(
        functools.partial(_kv_kernel, tm=tm),
        grid=(DEPTH, rows // tm),
        in_specs=[pl.BlockSpec((tm, D), lambda l, r: (r, 0)),
                  pl.BlockSpec((None, D, D), lambda l, r: (l, 0, 0)),
                  pl.BlockSpec((None, D, D), lambda l, r: (l, 0, 0))],
        out_specs=[pl.BlockSpec((1, tm * rpm, LANES), lambda l, r: (l, r, 0)),
                   pl.BlockSpec((1, tm * rpm, LANES), lambda l, r: (l, r, 0))],
        out_shape=[out, out],
        scratch_shapes=[pltpu.VMEM((D, D), BF16), pltpu.VMEM((D, D), BF16)],
        compiler_params=_params(("arbitrary", "arbitrary"), 40),
        name="mem_kv",
    )(mem2d, wk, wv)


MLP_CH = 1024


def _mlp_kernel(x_ref, g_ref, wup_ref, wdn_ref, gfin_ref, *rest, layer, n_inner, final_norm, emit_bf16):
    if emit_bf16:
        o_ref, wup_b, wdn_b, xn_sc, acc_sc = rest
        wup_b[...] = wup_ref[...].astype(BF16)
        wdn_b[...] = wdn_ref[...].astype(BF16)
    else:
        o_ref, xn_sc, acc_sc = rest
        wup_b, wdn_b = wup_ref, wdn_ref
    c = pl.program_id(1)

    @pl.when(c == 0)
    def _():
        x = x_ref[...]
        xn_sc[...] = _rms(x, g_ref[layer:layer + 1, :]).astype(BF16)
        acc_sc[...] = x

    xn = xn_sc[...]
    acc = acc_sc[...]
    for j in range(n_inner):
        h = _dot(xn, wup_b[:, j * MLP_CH:(j + 1) * MLP_CH])
        h = jnp.maximum(h, 0.0)
        acc = acc + _dot((h * h).astype(BF16), wdn_b[j * MLP_CH:(j + 1) * MLP_CH, :])
    acc_sc[...] = acc

    @pl.when(c == pl.num_programs(1) - 1)
    def _():
        if final_norm:
            o_ref[...] = _rms(acc, gfin_ref[...])
        else:
            o_ref[...] = acc


def _mlp_call(x2d, g, wup, wdn, gfin, layer, w_specs, extra_out_specs, extra_out_shapes, *, tm, hid_steps,
              final_norm):
    rows = x2d.shape[0]
    kern = functools.partial(_mlp_kernel, layer=layer, n_inner=D_FF // hid_steps // MLP_CH, final_norm=final_norm,
                             emit_bf16=bool(extra_out_specs))
    return pl.pallas_call(
        kern,
        grid=(rows // tm, hid_steps),
        in_specs=[pl.BlockSpec((tm, D), lambda r, c: (r, 0)), _const_spec((DEPTH, D))] + w_specs + [
                  _const_spec((1, D))],
        out_specs=[pl.BlockSpec((tm, D), lambda r, c: (r, 0))] + extra_out_specs,
        out_shape=[jax.ShapeDtypeStruct((rows, D), F32)] + extra_out_shapes,
        scratch_shapes=[pltpu.VMEM((tm, D), BF16), pltpu.VMEM((tm, D), F32)],
        compiler_params=_params(("arbitrary", "arbitrary"), 48),
        name="mlp",
    )(x2d, g, wup, wdn, gfin)


def _mlp_resident(x2d, g, wup_b, wdn_b, gfin, layer, *, tm, final_norm):
    w_specs = [_const_spec((D, D_FF)), _const_spec((D_FF, D))]
    return _mlp_call(x2d, g, wup_b, wdn_b, gfin, layer, w_specs, [], [], tm=tm, hid_steps=1,
                     final_norm=final_norm)[0]


def _mlp_streaming(x2d, g, wup, wdn, gfin, layer, *, tm, hid_steps, final_norm):
    hid_blk = D_FF // hid_steps
    w_specs = [pl.BlockSpec((None, D, hid_blk), lambda r, c: (layer, 0, c)),
               pl.BlockSpec((None, hid_blk, D), lambda r, c: (layer, c, 0))]
    b_specs = [pl.BlockSpec((D, hid_blk), lambda r, c: (0, c)), pl.BlockSpec((hid_blk, D), lambda r, c: (c, 0))]
    b_shapes = [jax.ShapeDtypeStruct((D, D_FF), BF16), jax.ShapeDtypeStruct((D_FF, D), BF16)]
    return _mlp_call(x2d, g, wup, wdn, gfin, layer, w_specs, b_specs, b_shapes, tm=tm, hid_steps=hid_steps,
                     final_norm=final_norm)


def _attn_prompt_kernel(x_ref, g_ref, wq_ref, k_ref, v_ref, wo_ref, o_ref, *, layer, tq, sub):
    starts = range(0, tq, sub)
    xs = [x_ref[0, r0:r0 + sub, :] for r0 in starts]
    qs = [(_dot(_rms(x, g_ref[layer:layer + 1, :]).astype(BF16), wq_ref[...]) * ATTN_SCALE).astype(BF16) for x in xs]
    outs = [[] for _ in starts]
    for h in range(HEADS):
        cols = slice(h * HD, (h + 1) * HD)
        kh = _load_head(k_ref, (), h)
        vh = _load_head(v_ref, (), h)
        for q, out in zip(qs, outs):
            s = lax.dot_general(q[:, cols], kh, NT_DIMS, preferred_element_type=F32)
            out.append(_dot(_softmax(s).astype(BF16), vh).astype(BF16))
    for r0, x, out in zip(starts, xs, outs):
        o_ref[0, r0:r0 + sub, :] = x + _dot(jnp.concatenate(out, axis=-1), wo_ref[...])


def _attn_prompt(x3d, g, wq, k4d, v4d, wo, layer, *, tq):
    b, l, _ = x3d.shape
    kv_spec = pl.BlockSpec((None, None, KV_ROWS, LANES), lambda i, t: (layer, i, 0, 0))
    return pl.pallas_call(
        functools.partial(_attn_prompt_kernel, layer=layer, tq=tq, sub=ATTN_SUB),
        grid=(b, l // tq),
        in_specs=[pl.BlockSpec((1, tq, D), lambda i, t: (i, t, 0)),
                  _const_spec((DEPTH, D)),
                  _const_spec((D, D)),
                  kv_spec, kv_spec,
                  _const_spec((D, D))],
        out_specs=pl.BlockSpec((1, tq, D), lambda i, t: (i, t, 0)),
        out_shape=jax.ShapeDtypeStruct(x3d.shape, F32),
        compiler_params=_params(("arbitrary", "arbitrary"), 56),
        name="attn_prompt",
    )(x3d, g, wq, k4d, v4d, wo)


def _attn_sample_kernel(x_ref, g_ref, wq_ref, k_hbm, v_hbm, wo_ref, o_ref, wq_b, wo_b, q_sc, o_sc, kbuf, vbuf, sem,
                        *, layer, sb, seq_len):
    i = pl.program_id(0)
    n_steps = pl.num_programs(0)

    def kv_copies(step, slot):
        seqs = pl.ds(step * sb, sb)
        return (pltpu.make_async_copy(k_hbm.at[layer, seqs], kbuf.at[slot], sem.at[0, slot]),
                pltpu.make_async_copy(v_hbm.at[layer, seqs], vbuf.at[slot], sem.at[1, slot]))

    @pl.when(i == 0)
    def _():
        for step in range(KV_RING - 1):
            for copy in kv_copies(step, step):
                copy.start()

    @pl.when(i + KV_RING - 1 < n_steps)
    def _():
        for copy in kv_copies(i + KV_RING - 1, (i + KV_RING - 1) % KV_RING):
            copy.start()

    @pl.when(i == 0)
    def _():
        wq_b[...] = wq_ref[...].astype(BF16)
        wo_b[...] = wo_ref[...].astype(BF16)
        xn = _rms(x_ref[...], g_ref[layer:layer + 1, :]).astype(BF16)
        q_sc[...] = _dot(xn, wq_b[...]) * ATTN_SCALE

    pair_rows = 2 * seq_len
    first_seq_row = lax.broadcasted_iota(jnp.int32, (pair_rows, HD), 0) < seq_len

    slot = i % KV_RING
    for copy in kv_copies(i, slot):
        copy.wait()
    k_ref = kbuf.at[slot]
    v_ref = vbuf.at[slot]

    for j in range(sb // 2):
        r0 = pl.multiple_of((i * sb + 2 * j) * seq_len, pair_rows)
        q8 = q_sc[pl.ds(r0, pair_rows), :].astype(BF16)
        scores = [lax.dot_general(q8[:, h * HD:(h + 1) * HD], _load_head(k_ref, (2 * j + s,), h), NT_DIMS,
                                  preferred_element_type=F32)
                  for s in range(2) for h in range(HEADS)]
        p = _softmax(jnp.concatenate(scores, axis=0))
        res = []
        for h in range(HEADS):
            oh = []
            for s in range(2):
                blk = (s * HEADS + h) * pair_rows
                oh.append(_dot(p[blk:blk + pair_rows, :].astype(BF16), _load_head(v_ref, (2 * j + s,), h)))
            res.append(jnp.where(first_seq_row, oh[0], oh[1]))
        o_sc[pl.ds(r0, pair_rows), :] = jnp.concatenate(res, axis=-1)

    @pl.when(i == pl.num_programs(0) - 1)
    def _():
        o_ref[...] = x_ref[...] + _dot(o_sc[...].astype(BF16), wo_b[...])


def _attn_sample(x2d, g, wq, k4d, v4d, wo, layer, *, sb, seq_len):
    rows = x2d.shape[0]
    nseq = rows // seq_len
    assert 2 * seq_len == 8 and nseq % sb == 0 and sb % 2 == 0
    assert nseq // sb >= KV_RING
    kern = functools.partial(_attn_sample_kernel, layer=layer, sb=sb, seq_len=seq_len)
    kv_spec = pl.BlockSpec(memory_space=pl.ANY)
    return pl.pallas_call(
        kern,
        grid=(nseq // sb,),
        in_specs=[_const_spec((rows, D)),
                  _const_spec((DEPTH, D)),
                  _layer_spec((D, D), layer),
                  kv_spec, kv_spec,
                  _layer_spec((D, D), layer)],
        out_specs=[pl.BlockSpec((rows, D), lambda i: (0, 0)),
                   pl.BlockSpec((D, D), lambda i: (0, 0)),
                   pl.BlockSpec((D, D), lambda i: (0, 0))],
        out_shape=[jax.ShapeDtypeStruct((rows, D), F32),
                   jax.ShapeDtypeStruct((D, D), BF16),
                   jax.ShapeDtypeStruct((D, D), BF16)],
        scratch_shapes=[pltpu.VMEM((rows, D), F32), pltpu.VMEM((rows, D), F32),
                        pltpu.VMEM((KV_RING, sb, KV_ROWS, LANES), F32),
                        pltpu.VMEM((KV_RING, sb, KV_ROWS, LANES), F32),
                        pltpu.SemaphoreType.DMA((2, KV_RING))],
        compiler_params=_params(("arbitrary",), 58),
        name="attn_sample",
    )(x2d, g, wq, k4d, v4d, wo)


def _glu_in(x, g, win_ref, bin_ref):
    xn = _rms(x, g).astype(BF16)
    h = _dot(xn, win_ref[...]) + bin_ref[...]
    return h[:, :D] * jax.nn.sigmoid(h[:, D:])


def _ln_silu_out(c, lng_ref, lnb_ref, wout_ref):
    mu = jnp.mean(c, axis=-1, keepdims=True)
    cc = c - mu
    var = jnp.mean(cc * cc, axis=-1, keepdims=True)
    n = cc * lax.rsqrt(var + EPS) * lng_ref[...] + lnb_ref[...]
    s = (n * jax.nn.sigmoid(n)).astype(BF16)
    return _dot(s, wout_ref[...])


def _conv_prompt_kernel(x_ref, g_ref, win_ref, bin_ref, dw_ref, dwb_ref, lng_ref, lnb_ref, wout_ref,
                        o_ref, hist_ref, gbuf, cbuf, *, tl, ns):
    n_tiles = D // LANES
    n_groups = tl // 8
    base = HIST_PAD - CONV_BUF

    def rows2(start, n):
        return pl.ds(2 * start, n, stride=2)

    @pl.when(pl.program_id(1) == 0)
    def _():
        for s in range(ns):
            for j in range(n_tiles):
                gbuf[s, j, rows2(0, HIST_PAD), :] = jnp.zeros((HIST_PAD, LANES), F32)

    xs = [x_ref[s] for s in range(ns)]
    xns = [_rms(x, g_ref[0:1, :]).astype(BF16) for x in xs]

    def glu_block(s, c):
        va = slice(c * 256, (c + 1) * 256)
        ga = slice(D + c * 256, D + (c + 1) * 256)
        val = _dot(xns[s], win_ref[:, va]) + bin_ref[:, va]
        gate = _dot(xns[s], win_ref[:, ga]) + bin_ref[:, ga]
        gl = val * jax.nn.sigmoid(gate)
        for jj in range(256 // LANES):
            gbuf[s, 2 * c + jj, rows2(HIST_PAD, tl), :] = gl[:, jj * LANES:(jj + 1) * LANES]

    def conv_tile(s, j):
        lanes = slice(j * LANES, (j + 1) * LANES)
        w = [jnp.broadcast_to(dw_ref[k:k + 1, lanes], (8, LANES)) for k in range(CONV_W)]
        bias = jnp.broadcast_to(dwb_ref[:, lanes], (8, LANES))
        accs = [bias] * 4
        for i in range(n_groups + 3):
            lags = range(max(0, i - n_groups + 1), min(i, 3) + 1)
            for r in range(8):
                taps = [(a, 8 * a + r) for a in lags if 8 * a + r < CONV_W]
                if taps:
                    wnd = gbuf[s, j, rows2(base + 8 * i + r, 8), :]
                    for a, k in taps:
                        accs[a] = accs[a] + w[k] * wnd
            if i >= 3:
                cbuf[s, (i - 3) * 8:(i - 2) * 8, lanes] = accs[3]
            accs = [bias, accs[0], accs[1], accs[2]]

    def finish(s):
        o_ref[s] = xs[s] + _ln_silu_out(cbuf[s], lng_ref, lnb_ref, wout_ref)
        for j in range(n_tiles):
            tail = gbuf[s, j, rows2(tl, HIST_PAD), :]
            hist_ref[s, :, j * LANES:(j + 1) * LANES] = tail
            gbuf[s, j, rows2(0, HIST_PAD), :] = tail

    for c in range(D // 256):
        glu_block(0, c)
    for s in range(ns):
        for j in range(n_tiles):
            conv_tile(s, j)
            if s + 1 < ns and j % 2 == 1:
                glu_block(s + 1, j // 2)
        finish(s)


def _conv_param_specs():
    return [_const_spec((DEPTH, D)),
            _layer_spec((D, 2 * D), 0),
            _const_spec((1, 2 * D)),
            _layer_spec((CONV_W, D), 0),
            _const_spec((1, D)),
            _const_spec((1, D)),
            _const_spec((1, D)),
            _layer_spec((D, D), 0)]


def _conv_prompt(x3d, conv_p, *, tl):
    b, l, _ = x3d.shape
    ns = CONV_NS
    assert b % ns == 0
    kern = functools.partial(_conv_prompt_kernel, tl=tl, ns=ns)
    return pl.pallas_call(
        kern,
        grid=(b // ns, l // tl),
        in_specs=[pl.BlockSpec((ns, tl, D), lambda i, t: (i, t, 0))] + _conv_param_specs(),
        out_specs=[pl.BlockSpec((ns, tl, D), lambda i, t: (i, t, 0)),
                   pl.BlockSpec((ns, HIST_PAD, D), lambda i, t: (i, 0, 0))],
        out_shape=[jax.ShapeDtypeStruct(x3d.shape, F32),
                   jax.ShapeDtypeStruct((b, HIST_PAD, D), F32)],
        scratch_shapes=[pltpu.VMEM((ns, D // LANES, 2 * (HIST_PAD + tl), LANES), F32),
                        pltpu.VMEM((ns, tl, D), F32)],
        compiler_params=_params(("arbitrary", "arbitrary"), 56),
        name="conv_prompt",
    )(x3d, *conv_p)


def _conv_sample_kernel(x_ref, hist_ref, g_ref, win_ref, bin_ref, dw_ref, dwb_ref, lng_ref, lnb_ref, wout_ref,
                        o_ref, nhist_ref, *, nb, seq_len):
    xt = jnp.concatenate([x_ref[:, t * D:(t + 1) * D] for t in range(seq_len)], axis=0)
    gl = _glu_in(xt, g_ref[0:1, :], win_ref, bin_ref)
    g_t = [gl[t * nb:(t + 1) * nb, :] for t in range(seq_len)]

    def padded(r, lanes):
        if r < CONV_BUF:
            return hist_ref[r, :, lanes]
        return g_t[r - CONV_BUF][:, lanes]

    cs = []
    for t in range(seq_len):
        tiles = []
        for j in range(D // LANES):
            lanes = slice(j * LANES, (j + 1) * LANES)
            acc = None
            for k in range(CONV_W):
                term = dw_ref[k:k + 1, lanes] * padded(t + k, lanes)
                acc = term if acc is None else acc + term
            tiles.append(acc)
        cs.append(jnp.concatenate(tiles, axis=-1))
    c = jnp.concatenate(cs, axis=0) + dwb_ref[...]
    res = _ln_silu_out(c, lng_ref, lnb_ref, wout_ref)
    for t in range(seq_len):
        o_ref[:, t * D:(t + 1) * D] = x_ref[:, t * D:(t + 1) * D] + res[t * nb:(t + 1) * nb, :]
    keep = CONV_BUF - seq_len
    nhist_ref[0:keep] = hist_ref[seq_len:CONV_BUF]
    for t in range(seq_len):
        nhist_ref[keep + t] = g_t[t]


def _conv_sample(xv, hist3, conv_p, *, nb, seq_len):
    nseq = xv.shape[0]
    assert seq_len <= CONV_BUF and nseq % nb == 0 and nb % 8 == 0
    kern = functools.partial(_conv_sample_kernel, nb=nb, seq_len=seq_len)
    hist_spec = pl.BlockSpec((CONV_BUF, nb, D), lambda i: (0, i, 0))
    return pl.pallas_call(
        kern,
        grid=(nseq // nb,),
        in_specs=[pl.BlockSpec((nb, seq_len * D), lambda i: (i, 0)), hist_spec] + _conv_param_specs(),
        out_specs=[pl.BlockSpec((nb, seq_len * D), lambda i: (i, 0)), hist_spec],
        out_shape=[jax.ShapeDtypeStruct(xv.shape, F32),
                   jax.ShapeDtypeStruct(hist3.shape, F32)],
        compiler_params=_params(("arbitrary",), 40),
        name="conv_sample",
    )(xv, hist3, *conv_p)


def _s5_scan_chunk_static(hs, st, are_ref, aim_ref, k, tl):
    ar = are_ref[:, k * S5_CW:(k + 1) * S5_CW]
    ai = aim_ref[:, k * S5_CW:(k + 1) * S5_CW]
    cre = slice(2 * k * S5_CW, (2 * k + 1) * S5_CW)
    cim = slice((2 * k + 1) * S5_CW, (2 * k + 2) * S5_CW)
    hr, hi = st[:, cre], st[:, cim]
    for t in range(tl):
        rows = slice(t * 8, (t + 1) * 8)
        hr, hi = ar * hr - ai * hi + hs[rows, cre], ar * hi + ai * hr + hs[rows, cim]
        hs[rows, cre] = hr
        hs[rows, cim] = hi
    st[:, cre] = hr
    st[:, cim] = hi


def _s5_scan(hs, st, are_ref, aim_ref, nb, tl):
    for k in range(S5_CHUNKS):
        ar = are_ref[:, k * S5_CW:(k + 1) * S5_CW]
        ai = aim_ref[:, k * S5_CW:(k + 1) * S5_CW]
        cre = slice(2 * k * S5_CW, (2 * k + 1) * S5_CW)
        cim = slice((2 * k + 1) * S5_CW, (2 * k + 2) * S5_CW)

        def seq_group(bg, carry, ar=ar, ai=ai, cre=cre, cim=cim):
            b0 = pl.multiple_of(bg * 8, 8)

            def step(t, h):
                hr, hi = h
                row = pl.multiple_of(t * nb + b0, 8)
                nr = ar * hr - ai * hi + hs[pl.ds(row, 8), cre]
                ni = ar * hi + ai * hr + hs[pl.ds(row, 8), cim]
                hs[pl.ds(row, 8), cre] = nr
                hs[pl.ds(row, 8), cim] = ni
                return nr, ni

            hr, hi = lax.fori_loop(0, tl, step, (st[pl.ds(b0, 8), cre], st[pl.ds(b0, 8), cim]),
                                   unroll=min(tl, 4))
            st[pl.ds(b0, 8), cre] = hr
            st[pl.ds(b0, 8), cim] = hi
            return carry

        lax.fori_loop(0, nb // 8, seq_group, 0)


def _s5_core(ut, ut_b, bblk_ref, are_ref, aim_ref, cblk_ref, d_ref, hs, st, nb, tl):
    for k in range(S5_CHUNKS):
        hs[:, 2 * k * S5_CW:(2 * k + 2) * S5_CW] = _dot(ut_b[:, k * 256:(k + 1) * 256], bblk_ref[k])
    if nb == 8:
        for k in range(S5_CHUNKS):
            _s5_scan_chunk_static(hs, st, are_ref, aim_ref, k, tl)
    else:
        _s5_scan(hs, st, are_ref, aim_ref, nb, tl)
    ys = [lax.dot_general(hs[:, 2 * k * S5_CW:(2 * k + 2) * S5_CW].astype(BF16), cblk_ref[k], NT_DIMS,
                          preferred_element_type=F32) for k in range(S5_CHUNKS)]
    y = jnp.concatenate(ys, axis=-1) + d_ref[...] * ut
    return jax.nn.gelu(y, approximate=True).astype(BF16)


def _s5_state_in(st, h0re_ref, h0im_ref):
    for k in range(S5_CHUNKS):
        st[:, 2 * k * S5_CW:(2 * k + 1) * S5_CW] = h0re_ref[:, k * S5_CW:(k + 1) * S5_CW]
        st[:, (2 * k + 1) * S5_CW:(2 * k + 2) * S5_CW] = h0im_ref[:, k * S5_CW:(k + 1) * S5_CW]


def _s5_state_out(st, hre_ref, him_ref):
    for k in range(S5_CHUNKS):
        hre_ref[:, k * S5_CW:(k + 1) * S5_CW] = st[:, 2 * k * S5_CW:(2 * k + 1) * S5_CW]
        him_ref[:, k * S5_CW:(k + 1) * S5_CW] = st[:, (2 * k + 1) * S5_CW:(2 * k + 2) * S5_CW]


def _glu_out(z):
    return z[:, :D] * jax.nn.sigmoid(z[:, D:])


def _s5_param_specs():
    return [_const_spec((DEPTH, D)),
            _const_spec((S5_CHUNKS, 256, 2 * S5_CW)),
            _const_spec((8, SSM_LANES)),
            _const_spec((8, SSM_LANES)),
            _const_spec((S5_CHUNKS, 256, 2 * S5_CW)),
            _const_spec((1, D)),
            _layer_spec((D, 2 * D), 0)]


def _s5_prompt_kernel(x_ref, g_ref, bblk_ref, are_ref, aim_ref, cblk_ref, d_ref, wglu_ref,
                      h0re_ref, h0im_ref, o_ref, hre_ref, him_ref, hs, st, ubuf, gtmp, *, nb, tl, n_sub):
    t = pl.program_id(0)
    rows = nb * tl
    n_tiles = D // LANES
    pitch = tl + 8

    @pl.when(t == 0)
    def _():
        _s5_state_in(st, h0re_ref, h0im_ref)

    def chunk_cols(k):
        return slice(2 * k * S5_CW, (2 * k + 2) * S5_CW)

    def c_proj(i, k):
        return lax.dot_general(hs[i, :, chunk_cols(k)].astype(BF16), cblk_ref[k], NT_DIMS,
                               preferred_element_type=F32)

    xs, uts = [], []
    for i in range(n_sub):
        x = x_ref[:, i * tl:(i + 1) * tl, :].reshape(rows, D)
        u = _rms(x, g_ref[1:2, :])
        for j in range(n_tiles):
            for b in range(nb):
                ubuf[i, j, b * pitch:b * pitch + tl, :] = u[b * tl:(b + 1) * tl, j * LANES:(j + 1) * LANES]
        ut = jnp.concatenate(
            [jnp.concatenate([ubuf[i, j, pl.ds(step, nb, stride=pitch), :] for j in range(n_tiles)], axis=1)
             for step in range(tl)], axis=0)
        ut_b = ut.astype(BF16)
        for k in range(S5_CHUNKS):
            hs[i, :, chunk_cols(k)] = _dot(ut_b[:, k * 256:(k + 1) * 256], bblk_ref[k])
        xs.append(x)
        uts.append(ut)

    ys = [[None] * S5_CHUNKS for _ in range(n_sub)]
    for i in range(n_sub):
        for k in range(S5_CHUNKS):
            _s5_scan_chunk_static(hs.at[i], st, are_ref, aim_ref, k, tl)
            if i > 0:
                ys[i - 1][k] = c_proj(i - 1, k)
    for k in range(S5_CHUNKS):
        ys[n_sub - 1][k] = c_proj(n_sub - 1, k)

    for i in range(n_sub):
        y = jnp.concatenate(ys[i], axis=-1) + d_ref[...] * uts[i]
        ge = jax.nn.gelu(y, approximate=True)
        for j in range(n_tiles):
            gtmp[i, j] = ge[:, j * LANES:(j + 1) * LANES]
        ge = jnp.concatenate(
            [jnp.concatenate([gtmp[i, j, pl.ds(b, tl, stride=nb), :] for j in range(n_tiles)], axis=1)
             for b in range(nb)], axis=0).astype(BF16)
        out = _glu_out(_dot(ge, wglu_ref[...]))
        o_ref[:, i * tl:(i + 1) * tl, :] = (xs[i] + out).reshape(nb, tl, D)

    @pl.when(t == pl.num_programs(0) - 1)
    def _():
        _s5_state_out(st, hre_ref, him_ref)


def _s5_prompt(x3d, s5_p, h0re, h0im, *, tl):
    nb, l, _ = x3d.shape
    assert nb % 8 == 0 and nb & (nb - 1) == 0 and tl & (tl - 1) == 0
    rows = nb * tl
    n_sub = S5_SUB
    step = n_sub * tl
    kern = functools.partial(_s5_prompt_kernel, nb=nb, tl=tl, n_sub=n_sub)
    st_shape = jax.ShapeDtypeStruct((nb, SSM_LANES), F32)
    return pl.pallas_call(
        kern,
        grid=(l // step,),
        in_specs=[pl.BlockSpec((nb, step, D), lambda t: (0, t, 0))] + _s5_param_specs() + [
                  _const_spec((nb, SSM_LANES)),
                  _const_spec((nb, SSM_LANES))],
        out_specs=[pl.BlockSpec((nb, step, D), lambda t: (0, t, 0)),
                   pl.BlockSpec((nb, SSM_LANES), lambda t: (0, 0)),
                   pl.BlockSpec((nb, SSM_LANES), lambda t: (0, 0))],
        out_shape=[jax.ShapeDtypeStruct(x3d.shape, F32), st_shape, st_shape],
        scratch_shapes=[pltpu.VMEM((n_sub, rows, 2 * SSM_LANES), F32), pltpu.VMEM((nb, 2 * SSM_LANES), F32),
                        pltpu.VMEM((n_sub, D // LANES, nb * (tl + 8), LANES), F32),
                        pltpu.VMEM((n_sub, D // LANES, rows, LANES), F32)],
        compiler_params=_params(("arbitrary",), 56),
        name="s5_prompt",
    )(x3d, *s5_p, h0re, h0im)


def _s5_sample_kernel(x_ref, g_ref, bblk_ref, are_ref, aim_ref, cblk_ref, d_ref, wglu_ref,
                      h0re_ref, h0im_ref, o_ref, hre_ref, him_ref, hs, st, *, nb, seq_len):
    _s5_state_in(st, h0re_ref, h0im_ref)
    xt = jnp.concatenate([x_ref[:, t * D:(t + 1) * D] for t in range(seq_len)], axis=0)
    ut = _rms(xt, g_ref[1:2, :])
    ge = _s5_core(ut, ut.astype(BF16), bblk_ref, are_ref, aim_ref, cblk_ref, d_ref, hs, st, nb, seq_len)
    out = _glu_out(_dot(ge, wglu_ref[...]))
    for t in range(seq_len):
        o_ref[:, t * D:(t + 1) * D] = x_ref[:, t * D:(t + 1) * D] + out[t * nb:(t + 1) * nb, :]
    _s5_state_out(st, hre_ref, him_ref)


def _s5_sample(xv, s5_p, h0re, h0im, *, nb, seq_len):
    nseq = xv.shape[0]
    assert nseq % nb == 0 and nb % 8 == 0
    rows = nb * seq_len
    kern = functools.partial(_s5_sample_kernel, nb=nb, seq_len=seq_len)
    st_shape = jax.ShapeDtypeStruct((nseq, SSM_LANES), F32)
    st_spec = pl.BlockSpec((nb, SSM_LANES), lambda i: (i, 0))
    return pl.pallas_call(
        kern,
        grid=(nseq // nb,),
        in_specs=[pl.BlockSpec((nb, seq_len * D), lambda i: (i, 0))] + _s5_param_specs() + [st_spec, st_spec],
        out_specs=[pl.BlockSpec((nb, seq_len * D), lambda i: (i, 0)), st_spec, st_spec],
        out_shape=[jax.ShapeDtypeStruct(xv.shape, F32), st_shape, st_shape],
        scratch_shapes=[pltpu.VMEM((rows, 2 * SSM_LANES), F32), pltpu.VMEM((nb, 2 * SSM_LANES), F32)],
        compiler_params=_params(("arbitrary",), 48),
        name="s5_sample",
    )(xv, *s5_p, h0re, h0im)


def _s5_discretize(a_re, a_im, log_dt, b_re, b_im, c_re, c_im):
    dt = jnp.exp(log_dt)[:, None]
    mag = jnp.exp(dt * a_re)
    ab_re = mag * jnp.cos(dt * a_im)
    ab_im = mag * jnp.sin(dt * a_im)
    den = a_re * a_re + a_im * a_im
    num_re = ab_re - 1.0
    coef_re = (num_re * a_re + ab_im * a_im) / den
    coef_im = (ab_im * a_re - num_re * a_im) / den
    bb_re = coef_re[..., None] * b_re - coef_im[..., None] * b_im
    bb_im = coef_re[..., None] * b_im + coef_im[..., None] * b_re
    gpc = SSM_GROUPS // S5_CHUNKS
    cols = 2 * S5_CW
    col_group = (lax.broadcasted_iota(jnp.int32, (gpc, 1, cols), 2) // SSM_STATE) % gpc
    own = col_group == lax.broadcasted_iota(jnp.int32, (gpc, 1, cols), 0)

    def block_diag(compact):
        full = jnp.where(own[None], compact[:, None], 0.0).astype(BF16)
        return full.reshape(S5_CHUNKS, gpc * SSM_GROUP, cols)

    bb = jnp.stack([bb_re, bb_im]).reshape(2, S5_CHUNKS, gpc, SSM_STATE, SSM_GROUP)
    bblk = block_diag(jnp.transpose(bb, (1, 4, 0, 2, 3)).reshape(S5_CHUNKS, SSM_GROUP, cols))
    cc = jnp.stack([c_re, -c_im]).reshape(2, S5_CHUNKS, gpc, SSM_GROUP, SSM_STATE)
    cblk = block_diag(jnp.transpose(cc, (1, 3, 0, 2, 4)).reshape(S5_CHUNKS, SSM_GROUP, cols))
    are = jnp.broadcast_to(ab_re.reshape(1, SSM_LANES), (8, SSM_LANES))
    aim = jnp.broadcast_to(ab_im.reshape(1, SSM_LANES), (8, SSM_LANES))
    return bblk, cblk, are, aim


S5_TL = 32
S5_SUB = 2
CONV_TL = 1024
CONV_NS = 1
KV_RING = 3
ATTN_TQ = 2048
ATTN_SUB = 256
MLP_TM = 1024


def kernel(x_prompt, x_sample, mem_prompt, cache_conv, state_s5_re, state_s5_im, cache_mem_k, cache_mem_v,
           norm_mix, norm_xattn, norm_ffn, norm_final,
           conv_w_in, conv_b_in, conv_dw, conv_dw_b, conv_ln_g, conv_ln_b, conv_w_out,
           s5_a_re, s5_a_im, s5_log_dt, s5_b_re, s5_b_im, s5_c_re, s5_c_im, s5_d, s5_w_glu,
           xattn_w_q, xattn_w_k, xattn_w_v, xattn_w_o, mlp_w_up, mlp_w_down):
    bp, lp, _ = x_prompt.shape
    bs, ls, _ = x_sample.shape

    conv_p = (norm_mix, conv_w_in.astype(BF16), conv_b_in, conv_dw, conv_dw_b, conv_ln_g, conv_ln_b,
              conv_w_out.astype(BF16))
    bblk, cblk, are, aim = _s5_discretize(s5_a_re[0], s5_a_im[0], s5_log_dt[0], s5_b_re[0], s5_b_im[0],
                                          s5_c_re[0], s5_c_im[0])
    s5_p = (norm_mix, bblk, are, aim, cblk, s5_d, s5_w_glu.astype(BF16))
    gfin = norm_final.reshape(1, D)

    kp, vp = _mem_kv(mem_prompt.reshape(bp * N_MEM, D), xattn_w_k, xattn_w_v)
    kp4 = kp.reshape(DEPTH, bp, KV_ROWS, LANES)
    vp4 = vp.reshape(DEPTH, bp, KV_ROWS, LANES)
    mem_k_prompt = _kv_rows_unview(kp4)
    mem_v_prompt = _kv_rows_unview(vp4)

    def attn_sample(x2d, i):
        return _attn_sample(x2d, norm_xattn, xattn_w_q, ck, cv, xattn_w_o, i, sb=4, seq_len=ls)

    def mlp_sample(x2d, i, final):
        return _mlp_streaming(x2d, norm_ffn, mlp_w_up, mlp_w_down, gfin, i, tm=min(MLP_TM, bs * ls), hid_steps=4,
                              final_norm=final)

    xp_conv, hist = _conv_prompt(x_prompt, conv_p, tl=CONV_TL)
    (xp_conv, hist), x_sample = lax.optimization_barrier(((xp_conv, hist), x_sample))

    ck = _kv_rows_view(cache_mem_k)
    cv = _kv_rows_view(cache_mem_v)
    hist3 = jnp.transpose(cache_conv[0], (1, 0, 2))
    xv, nhist = _conv_sample(x_sample.reshape(bs, ls * D), hist3, conv_p, nb=32, seq_len=ls)
    conv_sample = jnp.transpose(nhist, (1, 0, 2))[None]
    x, wq0, wo0 = attn_sample(xv.reshape(bs * ls, D), 0)
    x, wup0, wdn0 = mlp_sample(x, 0, False)
    xv, hre, him = _s5_sample(x.reshape(bs, ls * D), s5_p,
                              state_s5_re[0].reshape(bs, SSM_LANES), state_s5_im[0].reshape(bs, SSM_LANES),
                              nb=64, seq_len=ls)
    s5_re_sample = hre.reshape(1, bs, SSM_GROUPS, SSM_STATE)
    s5_im_sample = him.reshape(1, bs, SSM_GROUPS, SSM_STATE)
    x, wq1, wo1 = attn_sample(xv.reshape(bs * ls, D), 1)
    x, wup1, wdn1 = mlp_sample(x, 1, True)
    y_sample = x.reshape(bs, ls, D)

    def mlp_prompt(x3d, wup_b, wdn_b, i, final):
        y = _mlp_resident(x3d.reshape(bp * lp, D), norm_ffn, wup_b, wdn_b, gfin, i, tm=MLP_TM, final_norm=final)
        return y.reshape(bp, lp, D)

    conv_prompt = hist[None, :, HIST_PAD - CONV_BUF:, :]
    x = _attn_prompt(xp_conv, norm_xattn, wq0, kp4, vp4, wo0, 0, tq=ATTN_TQ)
    x = mlp_prompt(x, wup0, wdn0, 0, False)
    zeros_state = jnp.zeros((bp, SSM_LANES), F32)
    x, hre, him = _s5_prompt(x, s5_p, zeros_state, zeros_state, tl=S5_TL)
    s5_re_prompt = hre.reshape(1, bp, SSM_GROUPS, SSM_STATE)
    s5_im_prompt = him.reshape(1, bp, SSM_GROUPS, SSM_STATE)
    x = _attn_prompt(x, norm_xattn, wq1, kp4, vp4, wo1, 1, tq=ATTN_TQ)
    y_prompt = mlp_prompt(x, wup1, wdn1, 1, True)

    return (y_prompt, y_sample, conv_prompt, conv_sample, s5_re_prompt, s5_im_prompt,
            s5_re_sample, s5_im_sample, mem_k_prompt, mem_v_prompt)
```
